```python
import math
import jax, jax.numpy as jnp
from jax import lax
import numpy as np

D_MODEL = 1024
BATCH = 32
SEQ = 2048
DEPTH = 1
DEC_BATCH = 128
DEC_SEQ = 4
PAST_LEN = 16384
PAGE_SIZE = 128

MLA_HEADS = 8
MLA_NOPE = 64
MLA_ROPE = 32
MLA_V = 64
MLA_Q_RANK = 384
MLA_KV_RANK = 256
ROPE_THETA = 10000.0
MLA_SCALE = 1.0 / math.sqrt(MLA_NOPE + MLA_ROPE)
SB_HEADS = 8
SB_HEAD_DIM = 64
SB_SCALE = 1.0 / math.sqrt(SB_HEAD_DIM)
SB_WIDTH = SB_HEADS * SB_HEAD_DIM
MLA_WIDTH = MLA_HEADS * MLA_V
MIX_WIDTH = MLA_WIDTH + SB_WIDTH
IN_WIDTH = MLA_Q_RANK + MLA_KV_RANK + MLA_ROPE + 3 * SB_WIDTH
Q_BLOCK = 128
PEER_HEADS = 8
PEER_N_KEYS = 128
PEER_N_EXPERTS = PEER_N_KEYS * PEER_N_KEYS
PEER_TOPK = 16
PEER_QDIM = 256
PEER_CHUNK = 256
PLE_DIM = 256
ALPHA = (2 * DEPTH) ** 0.25
DEEPNORM_BETA = (8 * DEPTH) ** -0.25
EPS = 1e-6

kernel_name = "hybrid_mla_stickbreak_peer_decode_step"

F32 = jnp.float32


def _rmsnorm(x, g):
    xf = x.astype(F32)
    y = xf * lax.rsqrt(jnp.mean(xf * xf, axis=-1, keepdims=True) + EPS) * g.astype(F32)
    return y.astype(x.dtype)


def _layernorm(x, g, b):
    xf = x.astype(F32)
    mu = jnp.mean(xf, axis=-1, keepdims=True)
    xc = xf - mu
    var = jnp.mean(xc * xc, axis=-1, keepdims=True)
    return (xc * lax.rsqrt(var + EPS) * g.astype(F32) + b.astype(F32)).astype(x.dtype)


def _rope(x, pos):
    half = x.shape[-1] // 2
    inv = ROPE_THETA ** (-jnp.arange(half, dtype=F32) / half)
    ang = pos.astype(F32)[:, None] * inv[None, :]
    shape = (1, x.shape[1]) + (1,) * (x.ndim - 3) + (half,)
    cos = jnp.cos(ang).reshape(shape)
    sin = jnp.sin(ang).reshape(shape)
    x1 = x[..., :half].astype(F32)
    x2 = x[..., half:].astype(F32)
    return jnp.concatenate([x1 * cos - x2 * sin, x1 * sin + x2 * cos], axis=-1).astype(x.dtype)


def _mixer_inputs(x, pos, w_in, g_qn, g_kvn, w_uq):
    B, T, _ = x.shape
    cuts = [int(c) for c in np.cumsum([MLA_Q_RANK, MLA_KV_RANK, MLA_ROPE, SB_WIDTH, SB_WIDTH])]
    cq, ckv, kr, sq, sk, sv = jnp.split(x @ w_in, cuts, axis=-1)
    q = (_rmsnorm(cq, g_qn) @ w_uq).reshape(B, T, MLA_HEADS, MLA_NOPE + MLA_ROPE)
    q_nope = q[..., :MLA_NOPE]
    q_rope = _rope(q[..., MLA_NOPE:], pos)
    lat = _rmsnorm(ckv, g_kvn)
    k_rope = _rope(kr, pos)
    shp = (B, T, SB_HEADS, SB_HEAD_DIM)
    return q_nope, q_rope, lat, k_rope, sq.reshape(shp), sk.reshape(shp), sv.reshape(shp)


def _sb_block(q, k, v, mask, log_carry):
    z = jnp.einsum('bthd,bshd->bhts', q, k).astype(F32) * SB_SCALE
    L = jax.nn.log_sigmoid(-z)
    if mask is not None:
        L = jnp.where(mask, L, 0.0)
    suffix = lax.cumsum(L, axis=3, reverse=True) - L
    logA = jax.nn.log_sigmoid(z) + suffix
    if log_carry is not None:
        logA = logA + log_carry[..., None]
    w = jnp.exp(logA)
    if mask is not None:
        w = jnp.where(mask, w, 0.0)
    o = jnp.einsum('bhts,bshd->bthd', w, v)
    return o, jnp.sum(L, axis=-1)


def _mixers_prompt(q_nope, q_rope, lat, k_rope, sq, sk, sv, w_uk, w_uv):
    B, S = lat.shape[:2]
    k_nope = jnp.einsum('bsc,chd->bshd', lat, w_uk)
    v = jnp.einsum('bsc,chd->bshd', lat, w_uv)
    outs_a, outs_b = [], []
    for blk in range(S // Q_BLOCK):
        q0, ke = blk * Q_BLOCK, (blk + 1) * Q_BLOCK
        qpos = jnp.arange(q0, ke)
        kpos = jnp.arange(ke)
        s = (jnp.einsum('bthd,bshd->bhts', q_nope[:, q0:ke], k_nope[:, :ke])
             + jnp.einsum('bthr,bsr->bhts', q_rope[:, q0:ke], k_rope[:, :ke])).astype(F32) * MLA_SCALE
        p = jax.nn.softmax(jnp.where(qpos[:, None] >= kpos[None, :], s, -jnp.inf), axis=-1)
        outs_a.append(jnp.einsum('bhts,bshd->bthd', p, v[:, :ke]))
        o_sb, _ = _sb_block(sq[:, q0:ke], sk[:, :ke], sv[:, :ke], qpos[:, None] > kpos[None, :], None)
        outs_b.append(o_sb)
    o_a = jnp.concatenate(outs_a, axis=1).reshape(B, S, MLA_WIDTH)
    o_b = jnp.concatenate(outs_b, axis=1).reshape(B, S, SB_WIDTH)
    return jnp.concatenate([o_a, o_b], axis=-1).astype(lat.dtype)


def _mixers_sample(q_nope, q_rope, lat, k_rope, sq, sk, sv, w_uk, w_uv,
                   c_lat, c_kr, c_k, c_v, page_table, layer):
    B, T = lat.shape[:2]
    q_abs = jnp.einsum('bthd,chd->bthc', q_nope, w_uk)
    tpos = jnp.arange(T)

    def mla_scores(lat_k, kr_k):
        return (jnp.einsum('bthc,bsc->bhts', q_abs, lat_k)
                + jnp.einsum('bthr,bsr->bhts', q_rope, kr_k)).astype(F32) * MLA_SCALE

    s0 = jnp.where(tpos[:, None] >= tpos[None, :], mla_scores(lat, k_rope), -jnp.inf)
    m0 = jnp.max(s0, axis=-1)
    p0 = jnp.exp(s0 - m0[..., None])
    l0 = jnp.sum(p0, axis=-1)
    acc0 = jnp.einsum('bhts,bsc->bhtc', p0, lat).astype(F32)
    o0, ls0 = _sb_block(sq, sk, sv, tpos[:, None] > tpos[None, :], None)
    o0 = o0.astype(F32)

    def step(carry, phys):
        m, l, acc, o_sb, ls = carry
        lat_p = c_lat[layer, phys]
        kr_p = c_kr[layer, phys]
        k_p = c_k[layer, phys]
        v_p = c_v[layer, phys]
        s = mla_scores(lat_p, kr_p)
        m_new = jnp.maximum(m, jnp.max(s, axis=-1))
        corr = jnp.exp(m - m_new)
        p = jnp.exp(s - m_new[..., None])
        l = l * corr + jnp.sum(p, axis=-1)
        acc = acc * corr[..., None] + jnp.einsum('bhts,bsc->bhtc', p, lat_p).astype(F32)
        o_p, ls_p = _sb_block(sq, k_p, v_p, None, ls)
        return (m_new, l, acc, o_sb + o_p.astype(F32), ls + ls_p), None

    (m, l, acc, o_sb, ls), _ = lax.scan(step, (m0, l0, acc0, o0, ls0), page_table.T, reverse=True)
    o_mla = jnp.einsum('bhtc,chd->bthd', acc / l[..., None], w_uv).reshape(B, T, MLA_WIDTH)
    return jnp.concatenate([o_mla.astype(F32), o_sb.reshape(B, T, SB_WIDTH)], axis=-1).astype(lat.dtype)


def _peer(x, w_q, sub_keys, u_tab, v_tab):
    B, T, D = x.shape
    n = B * T
    c = min(PEER_CHUNK, n)
    n_pad = -(-n // c) * c
    tok = jnp.pad(x.reshape(n, D), ((0, n_pad - n), (0, 0)))

    def chunk(xc):
        q = (xc @ w_q).reshape(c, PEER_HEADS, 2, PEER_QDIM // 2)
        s = jnp.einsum('chjd,hjkd->chjk', q, sub_keys).astype(F32)
        sv, si = lax.top_k(s, PEER_TOPK)
        cand = sv[:, :, 0, :, None] + sv[:, :, 1, None, :]
        cidx = si[:, :, 0, :, None] * PEER_N_KEYS + si[:, :, 1, None, :]
        best, pos = lax.top_k(cand.reshape(c, PEER_HEADS, -1), PEER_TOPK)
        eidx = jnp.take_along_axis(cidx.reshape(c, PEER_HEADS, -1), pos, axis=-1)
        g = jax.nn.softmax(best, axis=-1)
        act = jax.nn.gelu(jnp.einsum('cd,chkd->chk', xc, u_tab[eidx]).astype(F32), approximate=False)
        return jnp.einsum('chk,chkd->cd', (g * act).astype(xc.dtype), v_tab[eidx])

    out = lax.map(chunk, tok.reshape(n_pad // c, c, D)).reshape(n_pad, D)[:n]
    return out.reshape(B, T, D)


def _finish(x, mix, p, w_out, ln1_g, ln1_b, pw_q, p_keys, p_u, p_v, ln2_g, ln2_b, ple_w, ple_g, ple_b):
    x = _layernorm(ALPHA * x + mix @ w_out, ln1_g, ln1_b)
    x = _layernorm(ALPHA * x + _peer(x, pw_q, p_keys, p_u, p_v), ln2_g, ln2_b)
    gate = jax.nn.sigmoid((x @ ple_g + ple_b).astype(F32))
    return x + ((p @ ple_w).astype(F32) * gate).astype(x.dtype)


def setup_inputs(seed: int = 0) -> dict:
    key = jax.random.key(seed)
    ks = jax.random.split(key, 32)
    n_pages = PAST_LEN // PAGE_SIZE
    used = DEC_BATCH * n_pages
    n_pool = used + max(1, used // 4)

    def nrm(k, shape, scale):
        return jax.random.normal(k, shape, F32) * scale

    col_scale = jnp.concatenate([jnp.ones((IN_WIDTH - SB_WIDTH,), F32),
                                 jnp.full((SB_WIDTH,), DEEPNORM_BETA, F32)])
    w_in = nrm(ks[0], (DEPTH, D_MODEL, IN_WIDTH), D_MODEL ** -0.5) * col_scale
    page_table = jax.random.permutation(ks[1], n_pool)[:used].reshape(DEC_BATCH, n_pages).astype(jnp.int32)
    return {
        "x_prompt": nrm(ks[2], (BATCH, SEQ, D_MODEL), 1.0),
        "x_sample": nrm(ks[3], (DEC_BATCH, DEC_SEQ, D_MODEL), 1.0),
        "cache_mla_latent": nrm(ks[4], (DEPTH, n_pool, PAGE_SIZE, MLA_KV_RANK), 1.0),
        "cache_mla_krope": nrm(ks[5], (DEPTH, n_pool, PAGE_SIZE, MLA_ROPE), 1.0),
        "cache_sb_k": nrm(ks[6], (DEPTH, n_pool, PAGE_SIZE, SB_HEADS, SB_HEAD_DIM), 1.0),
        "cache_sb_v": nrm(ks[7], (DEPTH, n_pool, PAGE_SIZE, SB_HEADS, SB_HEAD_DIM), DEEPNORM_BETA),
        "page_table": page_table,
        "p_prompt": nrm(ks[8], (DEPTH, BATCH, SEQ, PLE_DIM), 1.0),
        "p_sample": nrm(ks[9], (DEPTH, DEC_BATCH, DEC_SEQ, PLE_DIM), 1.0),
        "w_in": w_in,
        "g_q_norm": 1.0 + nrm(ks[10], (DEPTH, MLA_Q_RANK), 0.02),
        "g_kv_norm": 1.0 + nrm(ks[11], (DEPTH, MLA_KV_RANK), 0.02),
        "w_uq": nrm(ks[12], (DEPTH, MLA_Q_RANK, MLA_HEADS * (MLA_NOPE + MLA_ROPE)), MLA_Q_RANK ** -0.5),
        "w_uk": nrm(ks[13], (DEPTH, MLA_KV_RANK, MLA_HEADS, MLA_NOPE), MLA_KV_RANK ** -0.5),
        "w_uv": nrm(ks[14], (DEPTH, MLA_KV_RANK, MLA_HEADS, MLA_V), MLA_KV_RANK ** -0.5 * DEEPNORM_BETA),
        "w_out": nrm(ks[15], (DEPTH, MIX_WIDTH, D_MODEL), MIX_WIDTH ** -0.5 * DEEPNORM_BETA),
        "ln1_g": 1.0 + nrm(ks[16], (DEPTH, D_MODEL), 0.02),
        "ln1_b": nrm(ks[17], (DEPTH, D_MODEL), 0.02),
        "peer_w_q": nrm(ks[18], (DEPTH, D_MODEL, PEER_HEADS * PEER_QDIM), D_MODEL ** -0.5),
        "peer_sub_keys": nrm(ks[19], (DEPTH, PEER_HEADS, 2, PEER_N_KEYS, PEER_QDIM // 2), (PEER_QDIM // 2) ** -0.5),
        "peer_u": nrm(ks[20], (DEPTH, PEER_N_EXPERTS, D_MODEL), D_MODEL ** -0.5),
        "peer_v": nrm(ks[21], (DEPTH, PEER_N_EXPERTS, D_MODEL), DEEPNORM_BETA),
        "ln2_g": 1.0 + nrm(ks[22], (DEPTH, D_MODEL), 0.02),
        "ln2_b": nrm(ks[23], (DEPTH, D_MODEL), 0.02),
        "ple_w_proj": nrm(ks[24], (DEPTH, PLE_DIM, D_MODEL), PLE_DIM ** -0.5),
        "ple_w_gate": nrm(ks[25], (DEPTH, D_MODEL, D_MODEL), D_MODEL ** -0.5),
        "ple_b_gate": nrm(ks[26], (DEPTH, D_MODEL), 0.02),
    }


def reference(x_prompt, x_sample, cache_mla_latent, cache_mla_krope, cache_sb_k, cache_sb_v, page_table,
              p_prompt, p_sample, w_in, g_q_norm, g_kv_norm, w_uq, w_uk, w_uv, w_out, ln1_g, ln1_b,
              peer_w_q, peer_sub_keys, peer_u, peer_v, ln2_g, ln2_b, ple_w_proj, ple_w_gate, ple_b_gate):
    pos_p = jnp.arange(x_prompt.shape[1])
    past_len = page_table.shape[1] * PAGE_SIZE
    pos_s = past_len + jnp.arange(x_sample.shape[1])
    hp, hs = x_prompt, x_sample
    lat_p, kr_p, k_p, v_p = [], [], [], []
    lat_s, kr_s, k_s, v_s = [], [], [], []
    for i in range(DEPTH):
        tail = (w_out[i], ln1_g[i], ln1_b[i], peer_w_q[i], peer_sub_keys[i], peer_u[i], peer_v[i],
                ln2_g[i], ln2_b[i], ple_w_proj[i], ple_w_gate[i], ple_b_gate[i])
        qn, qr, lat, kr, sq, sk, sv = _mixer_inputs(hp, pos_p, w_in[i], g_q_norm[i], g_kv_norm[i], w_uq[i])
        mix = _mixers_prompt(qn, qr, lat, kr, sq, sk, sv, w_uk[i], w_uv[i])
        lat_p.append(lat); kr_p.append(kr); k_p.append(sk); v_p.append(sv)
        hp = _finish(hp, mix, p_prompt[i], *tail)
        qn, qr, lat, kr, sq, sk, sv = _mixer_inputs(hs, pos_s, w_in[i], g_q_norm[i], g_kv_norm[i], w_uq[i])
        mix = _mixers_sample(qn, qr, lat, kr, sq, sk, sv, w_uk[i], w_uv[i],
                             cache_mla_latent, cache_mla_krope, cache_sb_k, cache_sb_v, page_table, i)
        lat_s.append(lat); kr_s.append(kr); k_s.append(sk); v_s.append(sv)
        hs = _finish(hs, mix, p_sample[i], *tail)
    return (hp, hs, jnp.stack(lat_p), jnp.stack(kr_p), jnp.stack(k_p), jnp.stack(v_p),
            jnp.stack(lat_s), jnp.stack(kr_s), jnp.stack(k_s), jnp.stack(v_s))
```

```python
import functools
import math

import jax
import jax.numpy as jnp
import numpy as np
from jax import lax
from jax.experimental import pallas as pl
from jax.experimental.pallas import tpu as pltpu

F32 = jnp.float32
BF16 = jnp.bfloat16

ROPE_THETA = 10000.0
EPS = 1e-6
PEER_TOPK = 16
LANES = 128
VMEM_LIMIT = 56 * 1024 * 1024
NEG_INF = float("-inf")


def _cparams(sem):
    return pltpu.CompilerParams(dimension_semantics=sem, vmem_limit_bytes=VMEM_LIMIT)


def _full(shape):
    n = len(shape)
    return pl.BlockSpec(shape, lambda *_: (0,) * n)


def _dot(a, b):
    return jnp.dot(a, b, preferred_element_type=F32)


def _dot_nt(a, b):
    return lax.dot_general(a, b, (((1,), (1,)), ((), ())), preferred_element_type=F32)


def _layernorm(x, g, b):
    mu = jnp.mean(x, axis=-1, keepdims=True)
    xc = x - mu
    var = jnp.mean(xc * xc, axis=-1, keepdims=True)
    return xc * lax.rsqrt(var + EPS) * g + b


def _rmsnorm(x, g):
    return x * lax.rsqrt(jnp.mean(x * x, axis=-1, keepdims=True) + EPS) * g


def _neg_softplus(z):
    return -(jnp.maximum(z, 0.0) + jnp.log(1.0 + jnp.exp(-jnp.abs(z))))


def _split_hi_lo(x):
    hi = x.astype(BF16)
    lo = (x - hi.astype(F32)).astype(BF16)
    return hi, lo


def _inproj_kernel(x_ref, tab_ref, w1_ref, gq_ref, gkv_ref, wq_ref, wkv_ref, e_ref,
                   lat_ref, kr_ref, sk_ref, sv_ref, qpad_ref, kpad_ref, vb_ref, sqb_ref, skb_ref,
                   svb_ref, *, q_rank, kv_rank, sb_width, rope, heads, sb_scale):
    xb = x_ref[...].astype(BF16)
    proj = _dot(xb, w1_ref[...])
    c1 = q_rank
    c2 = c1 + kv_rank
    c3 = c2 + sb_width
    c4 = c3 + sb_width
    c5 = c4 + sb_width
    cq = proj[:, :c1]
    ckv = proj[:, c1:c2]
    sq = proj[:, c2:c3]
    sk = proj[:, c3:c4]
    sv = proj[:, c4:c5]
    kr_a = proj[:, c5:c5 + LANES]
    kr_b = proj[:, c5 + LANES:c5 + 2 * LANES]

    tab = tab_ref[...]
    cq_t = tab[:, 0:LANES]
    sq_t = tab[:, LANES:2 * LANES]
    ck_t = tab[:, 2 * LANES:3 * LANES]
    sk_t = tab[:, 3 * LANES:4 * LANES]

    cqn = _rmsnorm(cq, gq_ref[...]).astype(BF16)
    qq = _dot(cqn, wq_ref[...])
    hw = heads * LANES
    for h in range(heads):
        sl = slice(h * LANES, (h + 1) * LANES)
        sl2 = slice(hw + h * LANES, hw + (h + 1) * LANES)
        qpad_ref[:, sl] = (qq[:, sl] * cq_t + qq[:, sl2] * sq_t).astype(BF16)

    lat = _rmsnorm(ckv, gkv_ref[...])
    lat_ref[...] = lat
    kv = _dot(lat.astype(BF16), wkv_ref[...])
    krot = kr_a * ck_t + kr_b * sk_t
    kr_ref[...] = krot[:, :rope]
    kpad_ref[...] = (kv[:, :hw] + _dot(krot.astype(BF16), e_ref[...])).astype(BF16)
    vb_ref[...] = kv[:, hw:].astype(BF16)
    sqb_ref[...] = (sq * sb_scale).astype(BF16)
    skb_ref[...] = sk.astype(BF16)
    svb_ref[...] = sv.astype(BF16)
    sk_ref[...] = sk
    sv_ref[...] = sv


def _inproj(x, tab, w1, gq, gkv, wq, wkv, emat, *, dims, tm):
    n, d = x.shape
    heads, nope, rope, vdim = dims["heads"], dims["nope"], dims["rope"], dims["vdim"]
    q_rank, kv_rank, sb_width = dims["q_rank"], dims["kv_rank"], dims["sb_width"]
    hw = heads * LANES
    period = tab.shape[0] // tm
    row = lambda i: (i, 0)
    out_shapes = (
        jax.ShapeDtypeStruct((n, kv_rank), F32),
        jax.ShapeDtypeStruct((n, rope), F32),
        jax.ShapeDtypeStruct((n, sb_width), F32),
        jax.ShapeDtypeStruct((n, sb_width), F32),
        jax.ShapeDtypeStruct((n, hw), BF16),
        jax.ShapeDtypeStruct((n, hw), BF16),
        jax.ShapeDtypeStruct((n, heads * vdim), BF16),
        jax.ShapeDtypeStruct((n, sb_width), BF16),
        jax.ShapeDtypeStruct((n, sb_width), BF16),
        jax.ShapeDtypeStruct((n, sb_width), BF16),
    )
    kern = functools.partial(_inproj_kernel, q_rank=q_rank, kv_rank=kv_rank, sb_width=sb_width,
                             rope=rope, heads=heads, sb_scale=dims["sb_scale"])
    return pl.pallas_call(
        kern,
        grid=(n // tm,),
        in_specs=[
            pl.BlockSpec((tm, d), row),
            pl.BlockSpec((tm, 4 * LANES), lambda i: (i % period, 0)),
            _full(w1.shape), _full(gq.shape), _full(gkv.shape), _full(wq.shape), _full(wkv.shape),
            _full(emat.shape),
        ],
        out_specs=tuple(pl.BlockSpec((tm, s.shape[1]), row) for s in out_shapes),
        out_shape=out_shapes,
        compiler_params=_cparams(("parallel",)),
        name="inproj",
    )(x, tab, w1, gq, gkv, wq, wkv, emat)


def _prompt_attn_kernel(q_ref, k_ref, v_ref, sq_ref, sk_ref, sv_ref, tri_ref, o_ref, *,
                        blk, heads, vdim, sb_heads, sb_dim):
    i = pl.program_id(1)
    row = lax.broadcasted_iota(jnp.int32, (blk, blk), 0)
    col = lax.broadcasted_iota(jnp.int32, (blk, blk), 1)
    tri = tri_ref[...]
    mla_w = heads * vdim

    for h in range(heads):
        qh = q_ref[:, h * LANES:(h + 1) * LANES]

        def mla_step(j, carry, masked, qh=qh, h=h):
            m, l, acc = carry
            off = pl.multiple_of((i - j) * blk, blk)
            kh = k_ref[pl.ds(off, blk), h * LANES:(h + 1) * LANES]
            vh = v_ref[pl.ds(off, blk), h * vdim:(h + 1) * vdim]
            s = _dot_nt(qh, kh)
            if masked:
                s = jnp.where(row >= col, s, NEG_INF)
            m_new = jnp.maximum(m, jnp.max(s, axis=-1, keepdims=True))
            corr = jnp.exp(m - m_new)
            p = jnp.exp(s - m_new)
            l = l * corr + jnp.sum(p, axis=-1, keepdims=True)
            acc = acc * corr + _dot(p.astype(BF16), vh)
            return m_new, l, acc

        init = (jnp.full((blk, 1), NEG_INF, F32), jnp.zeros((blk, 1), F32), jnp.zeros((blk, vdim), F32))
        carry = mla_step(0, init, True)
        m, l, acc = lax.fori_loop(1, i + 1, functools.partial(mla_step, masked=False), carry)
        o_ref[:, h * vdim:(h + 1) * vdim] = (acc / l).astype(o_ref.dtype)

    for h in range(sb_heads):
        sqh = sq_ref[:, h * sb_dim:(h + 1) * sb_dim]

        def sb_step(j, carry, masked, sqh=sqh, h=h):
            log_carry, acc = carry
            off = pl.multiple_of((i - j) * blk, blk)
            kh = sk_ref[pl.ds(off, blk), h * sb_dim:(h + 1) * sb_dim]
            vh = sv_ref[pl.ds(off, blk), h * sb_dim:(h + 1) * sb_dim]
            z = _dot_nt(sqh, kh)
            lneg = _neg_softplus(z)
            lpos = z + lneg
            if masked:
                lneg = jnp.where(row > col, lneg, 0.0)
            hi, lo = _split_hi_lo(lneg)
            suffix = _dot(hi, tri) + _dot(lo, tri)
            w = jnp.exp(lpos + suffix + log_carry)
            if masked:
                w = jnp.where(row > col, w, 0.0)
            acc = acc + _dot(w.astype(BF16), vh)
            log_carry = log_carry + jnp.sum(lneg, axis=-1, keepdims=True)
            return log_carry, acc

        init = (jnp.zeros((blk, 1), F32), jnp.zeros((blk, sb_dim), F32))
        carry = sb_step(0, init, True)
        _, acc = lax.fori_loop(1, i + 1, functools.partial(sb_step, masked=False), carry)
        o_ref[:, mla_w + h * sb_dim:mla_w + (h + 1) * sb_dim] = acc.astype(o_ref.dtype)


def _strict_lower(n):
    r = np.arange(n)
    return jnp.asarray((r[:, None] > r[None, :]).astype(np.float32), dtype=BF16)


def _prompt_attn(qpad, kpad, vb, sqb, skb, svb, *, batch, seq, dims, blk):
    heads, vdim = dims["heads"], dims["vdim"]
    sb_heads, sb_dim = dims["sb_heads"], dims["sb_dim"]
    nq = seq // blk
    hw = heads * LANES
    mix_w = heads * vdim + sb_heads * sb_dim
    qrow = lambda b, i: (b * nq + i, 0)
    per_b = lambda b, i: (b, 0, 0)
    kern = functools.partial(_prompt_attn_kernel, blk=blk, heads=heads, vdim=vdim,
                             sb_heads=sb_heads, sb_dim=sb_dim)
    return pl.pallas_call(
        kern,
        grid=(batch, nq),
        in_specs=[
            pl.BlockSpec((blk, hw), qrow),
            pl.BlockSpec((None, seq, hw), per_b),
            pl.BlockSpec((None, seq, heads * vdim), per_b),
            pl.BlockSpec((blk, sb_heads * sb_dim), qrow),
            pl.BlockSpec((None, seq, sb_heads * sb_dim), per_b),
            pl.BlockSpec((None, seq, sb_heads * sb_dim), per_b),
            _full((blk, blk)),
        ],
        out_specs=pl.BlockSpec((blk, mix_w), qrow),
        out_shape=jax.ShapeDtypeStruct((batch * seq, mix_w), BF16),
        compiler_params=_cparams(("parallel", "arbitrary")),
        name="prompt_attn",
    )(qpad, kpad.reshape(batch, seq, hw), vb.reshape(batch, seq, -1), sqb,
      skb.reshape(batch, seq, -1), svb.reshape(batch, seq, -1), _strict_lower(blk))


def _decode_kernel(pt_ref, qpad_ref, sqb_ref, nlat_ref, nkr_ref, nk_ref, nv_ref,
                   wuk_ref, sel_ref, wuv_ref, tri_ref,
                   clat_hbm, ckr_hbm, ck_hbm, cv_hbm,
                   o_ref,
                   lat_buf, kr_buf, k_buf, v_buf, sems, *,
                   t_new, heads, vdim, sb_heads, sb_dim, page, group, n_pages, kv_rank):
    b = pl.program_id(0)
    nb = pl.num_programs(0)
    n_groups = n_pages // group
    rows = t_new * heads
    chunk = LANES

    def copies(bb, g, slot):
        out = []
        base = n_pages - (g + 1) * group
        for p in range(group):
            phys = pt_ref[bb, base + p]
            dst = pl.ds(p * page, page)
            out.append(pltpu.make_async_copy(clat_hbm.at[phys], lat_buf.at[slot, dst], sems.at[0, slot]))
            out.append(pltpu.make_async_copy(ckr_hbm.at[phys], kr_buf.at[slot, dst], sems.at[1, slot]))
            out.append(pltpu.make_async_copy(ck_hbm.at[phys], k_buf.at[slot, dst], sems.at[2, slot]))
            out.append(pltpu.make_async_copy(cv_hbm.at[phys], v_buf.at[slot, dst], sems.at[3, slot]))
        return out

    @pl.when(b == 0)
    def _():
        for c in copies(0, 0, 0):
            c.start()

    def per_head_rows(x, width):
        x = x.astype(F32)
        full = jnp.concatenate(
            [jnp.broadcast_to(x[t:t + 1, :], (heads, x.shape[1])) for t in range(t_new)], axis=0)
        r = lax.broadcasted_iota(jnp.int32, full.shape, 0)
        c = lax.broadcasted_iota(jnp.int32, full.shape, 1)
        return jnp.where((c // width) == (r % heads), full, 0.0).astype(BF16)

    rope = kr_buf.shape[-1]
    q_bd = per_head_rows(qpad_ref[...], LANES)
    q_abs = _dot_nt(q_bd, wuk_ref[...]).astype(BF16)
    q_rope = _dot(q_bd, sel_ref[...])[:, :rope].astype(BF16)
    sq_bd = per_head_rows(sqb_ref[...], sb_dim)
    tri = tri_ref[...]

    def process(lat, kr, kk, vv, carry, mask_mla=None, mask_sb=None):
        m, l, acc, o_sb, ls = carry
        nk = lat.shape[0]
        s = _dot_nt(q_abs, lat) + _dot_nt(q_rope, kr)
        if mask_mla is not None:
            s = jnp.where(mask_mla, s, NEG_INF)
        m_new = jnp.maximum(m, jnp.max(s, axis=-1, keepdims=True))
        corr = jnp.exp(m - m_new)
        p = jnp.exp(s - m_new)
        l = l * corr + jnp.sum(p, axis=-1, keepdims=True)
        acc = acc * corr + _dot(p.astype(BF16), lat)

        z = _dot_nt(sq_bd, kk)
        lneg = _neg_softplus(z)
        lpos = z + lneg
        if mask_sb is not None:
            lneg = jnp.where(mask_sb, lneg, 0.0)
        nch = nk // chunk
        hi, lo = _split_hi_lo(lneg)
        stacked = jnp.concatenate([hi[:, c * chunk:(c + 1) * chunk] for c in range(nch)]
                                  + [lo[:, c * chunk:(c + 1) * chunk] for c in range(nch)], axis=0)
        suf = _dot(stacked, tri)
        pieces = [None] * nch
        run = ls
        for c in range(nch - 1, -1, -1):
            within = suf[c * rows:(c + 1) * rows] + suf[(nch + c) * rows:(nch + c + 1) * rows]
            pieces[c] = within + run
            run = run + jnp.sum(lneg[:, c * chunk:(c + 1) * chunk], axis=-1, keepdims=True)
        suffix = pieces[0] if nch == 1 else jnp.concatenate(pieces, axis=1)
        w = jnp.exp(lpos + suffix)
        if mask_sb is not None:
            w = jnp.where(mask_sb, w, 0.0)
        o_sb = o_sb + _dot(w.astype(BF16), vv)
        return m_new, l, acc, o_sb, run

    r_i = lax.broadcasted_iota(jnp.int32, (rows, chunk), 0) // heads
    c_i = lax.broadcasted_iota(jnp.int32, (rows, chunk), 1)
    valid = c_i < t_new
    init = (jnp.full((rows, 1), NEG_INF, F32), jnp.zeros((rows, 1), F32), jnp.zeros((rows, kv_rank), F32),
            jnp.zeros((rows, sb_heads * sb_dim), F32), jnp.zeros((rows, 1), F32))
    carry = process(nlat_ref[...].astype(BF16), nkr_ref[...].astype(BF16), nk_ref[...].astype(BF16),
                    nv_ref[...].astype(BF16), init,
                    mask_mla=valid & (r_i >= c_i), mask_sb=valid & (r_i > c_i))

    def group_step(g, carry):
        slot = g % 2
        nxt = 1 - slot

        @pl.when(g + 1 < n_groups)
        def _():
            for c in copies(b, g + 1, nxt):
                c.start()

        @pl.when((g + 1 == n_groups) & (b + 1 < nb))
        def _():
            for c in copies(b + 1, 0, nxt):
                c.start()

        for c in copies(b, g, slot):
            c.wait()
        return process(lat_buf[slot].astype(BF16), kr_buf[slot].astype(BF16),
                       k_buf[slot].astype(BF16), v_buf[slot].astype(BF16), carry)

    m, l, acc, o_sb, _ = lax.fori_loop(0, n_groups, group_step, carry)

    o_all = _dot((acc / l).astype(BF16), wuv_ref[...])

    def store_own_head(x, width, lane0):
        r = lax.broadcasted_iota(jnp.int32, x.shape, 0)
        c = lax.broadcasted_iota(jnp.int32, x.shape, 1)
        xm = jnp.where((c // width) == (r % heads), x, 0.0)
        for t in range(t_new):
            o_ref[t:t + 1, lane0:lane0 + heads * width] = jnp.sum(
                xm[t * heads:(t + 1) * heads], axis=0, keepdims=True)

    store_own_head(o_all, vdim, 0)
    store_own_head(o_sb, sb_dim, heads * vdim)


def _decode_attn(qpad, sqb, lat, kr, sk, sv, wuk_pad, sel, wuv2, caches, page_table, *,
                 dec_batch, t_new, dims, group):
    heads, vdim = dims["heads"], dims["vdim"]
    sb_heads, sb_dim = dims["sb_heads"], dims["sb_dim"]
    kv_rank, rope = dims["kv_rank"], dims["rope"]
    sb_width = sb_heads * sb_dim
    c_lat, c_kr, c_k, c_v = caches
    page = c_lat.shape[1]
    n_pages = page_table.shape[1]
    assert heads == sb_heads and n_pages % group == 0 and (n_pages // group) % 2 == 0
    hw = heads * LANES
    mix_w = heads * vdim + sb_width

    def pad_new(x, width):
        x = x.reshape(dec_batch, t_new, -1)
        return jnp.pad(x, ((0, 0), (0, LANES - t_new), (0, width - x.shape[-1])))

    nlat = pad_new(lat, kv_rank)
    nkr = pad_new(kr, rope)
    nk = pad_new(sk, sb_width)
    nv = pad_new(sv, sb_width)
    per_b = lambda b, pt: (b, 0, 0)
    const2 = lambda b, pt: (0, 0)
    kern = functools.partial(_decode_kernel, t_new=t_new, heads=heads, vdim=vdim, sb_heads=sb_heads,
                             sb_dim=sb_dim, page=page, group=group, n_pages=n_pages, kv_rank=kv_rank)
    gk = group * page
    grid_spec = pltpu.PrefetchScalarGridSpec(
        num_scalar_prefetch=1,
        grid=(dec_batch,),
        in_specs=[
            pl.BlockSpec((None, t_new, hw), per_b),
            pl.BlockSpec((None, t_new, sb_width), per_b),
            pl.BlockSpec((None, LANES, kv_rank), per_b),
            pl.BlockSpec((None, LANES, rope), per_b),
            pl.BlockSpec((None, LANES, sb_width), per_b),
            pl.BlockSpec((None, LANES, sb_width), per_b),
            pl.BlockSpec(wuk_pad.shape, const2),
            pl.BlockSpec(sel.shape, const2),
            pl.BlockSpec(wuv2.shape, const2),
            pl.BlockSpec((LANES, LANES), const2),
            pl.BlockSpec(memory_space=pl.ANY),
            pl.BlockSpec(memory_space=pl.ANY),
            pl.BlockSpec(memory_space=pl.ANY),
            pl.BlockSpec(memory_space=pl.ANY),
        ],
        out_specs=pl.BlockSpec((None, t_new, mix_w), per_b),
        scratch_shapes=[
            pltpu.VMEM((2, gk, kv_rank), F32),
            pltpu.VMEM((2, gk, rope), F32),
            pltpu.VMEM((2, gk, sb_width), F32),
            pltpu.VMEM((2, gk, sb_width), F32),
            pltpu.SemaphoreType.DMA((4, 2)),
        ],
    )
    out = pl.pallas_call(
        kern,
        grid_spec=grid_spec,
        out_shape=jax.ShapeDtypeStruct((dec_batch, t_new, mix_w), F32),
        compiler_params=_cparams(("arbitrary",)),
        name="decode_attn",
    )(page_table, qpad.reshape(dec_batch, t_new, hw), sqb.reshape(dec_batch, t_new, sb_width),
      nlat, nkr, nk, nv, wuk_pad, sel, wuv2, _strict_lower(LANES), c_lat, c_kr, c_k, c_v)
    return out.reshape(dec_batch * t_new, mix_w)


def _finish_a_kernel(x_ref, mix_ref, wout_ref, g_ref, b_ref, wpq_ref, keys_ref,
                     h1_ref, h1t_ref, st_ref, stats_ref, v0_ref, v1_ref, *, alpha, p_heads, n_keys, topk):
    h1 = _layernorm(alpha * x_ref[...] + _dot(mix_ref[...].astype(BF16), wout_ref[...]),
                    g_ref[...], b_ref[...])
    h1_ref[...] = h1
    h1t_ref[...] = h1.T.astype(BF16)
    qp = _dot(h1.astype(BF16), wpq_ref[...]).astype(BF16)
    qhalf = qp.shape[1] // (2 * p_heads)

    def top_vals(s, dst_ref):
        cur = s
        for r in range(topk):
            mx = jnp.max(cur, axis=0, keepdims=True)
            dst_ref[r:r + 1, :] = mx
            cur = jnp.where(cur == mx, NEG_INF, cur)

    for h in range(p_heads):
        s0 = _dot_nt(keys_ref[2 * h], qp[:, (2 * h) * qhalf:(2 * h + 1) * qhalf])
        s1 = _dot_nt(keys_ref[2 * h + 1], qp[:, (2 * h + 1) * qhalf:(2 * h + 2) * qhalf])
        st_ref[2 * h] = s0
        st_ref[2 * h + 1] = s1
        top_vals(s0, v0_ref)
        top_vals(s1, v1_ref)
        v0 = v0_ref[...]
        v1 = v1_ref[...]
        half = topk // 2
        parts = [v0[0:1, :] + v1]
        for a in range(1, half):
            parts.append(v0[a:a + 1, :] + v1[0:half, :])
        parts.append(v0[half:topk, :] + v1[0:1, :])
        cand = jnp.concatenate(parts, axis=0)
        cur = cand
        tau = None
        for r in range(topk):
            tau = jnp.max(cur, axis=0, keepdims=True)
            cur = jnp.where(cur == tau, NEG_INF, cur)
        m0 = v0[0:1, :]
        m1 = v1[0:1, :]
        zsum = jnp.sum(jnp.where(cand >= tau, jnp.exp(cand - (m0 + m1)), 0.0), axis=0, keepdims=True)
        stats_ref[h:h + 1, :] = tau
        stats_ref[p_heads + h:p_heads + h + 1, :] = m0
        stats_ref[2 * p_heads + h:2 * p_heads + h + 1, :] = m1
        stats_ref[3 * p_heads + h:3 * p_heads + h + 1, :] = 1.0 / zsum


def _finish_a(x, mix, wout, g, b, wpq, keys, *, alpha, tm):
    n, d = x.shape
    n2, n_keys, _ = keys.shape
    p_heads = n2 // 2
    row = lambda i: (i, 0)
    kern = functools.partial(_finish_a_kernel, alpha=alpha, p_heads=p_heads, n_keys=n_keys, topk=PEER_TOPK)
    return pl.pallas_call(
        kern,
        grid=(n // tm,),
        in_specs=[pl.BlockSpec((tm, d), row), pl.BlockSpec((tm, mix.shape[1]), row), _full(wout.shape),
                  _full(g.shape), _full(b.shape), _full(wpq.shape), _full(keys.shape)],
        out_specs=(pl.BlockSpec((tm, d), row),
                   pl.BlockSpec((d, tm), lambda i: (0, i)),
                   pl.BlockSpec((n2, n_keys, tm), lambda i: (0, 0, i)),
                   pl.BlockSpec((4 * p_heads, tm), lambda i: (0, i))),
        out_shape=(jax.ShapeDtypeStruct((n, d), F32),
                   jax.ShapeDtypeStruct((d, n), BF16),
                   jax.ShapeDtypeStruct((n2, n_keys, n), F32),
                   jax.ShapeDtypeStruct((4 * p_heads, n), F32)),
        scratch_shapes=[pltpu.VMEM((PEER_TOPK, tm), F32), pltpu.VMEM((PEER_TOPK, tm), F32)],
        compiler_params=_cparams(("parallel",)),
        name="finish_a",
    )(x, mix, wout, g, b, wpq, keys)


def _peer_dense_kernel(h1t_ref, st_ref, stats_ref, u_ref, vt_ref, o_ref,
                       thr_ref, sc_ref, e1_ref, p_ref, *, p_heads, n_keys, et, tt):
    e = pl.program_id(1)

    @pl.when(e == 0)
    def _():
        o_ref[...] = jnp.zeros_like(o_ref)
        for h in range(p_heads):
            tau = stats_ref[h:h + 1, :]
            m0 = stats_ref[p_heads + h:p_heads + h + 1, :]
            m1 = stats_ref[2 * p_heads + h:2 * p_heads + h + 1, :]
            zinv = stats_ref[3 * p_heads + h:3 * p_heads + h + 1, :]
            s0 = st_ref[2 * h]
            thr_ref[h] = tau - s0
            sc_ref[h] = jnp.exp(s0 - m0) * zinv
            e1_ref[h] = jnp.exp(st_ref[2 * h + 1] - m1)

    st = _dot(u_ref[...], h1t_ref[...])
    per_tile = et // n_keys
    inv_sqrt2 = 1.0 / math.sqrt(2.0)
    i0 = pl.multiple_of(e * per_tile, per_tile)
    for c in range(tt // LANES):
        cs = slice(c * LANES, (c + 1) * LANES)
        thr_rows = [thr_ref[h, pl.ds(i0, per_tile), cs] for h in range(p_heads)]
        sc_rows = [sc_ref[h, pl.ds(i0, per_tile), cs] for h in range(p_heads)]
        for ii in range(per_tile):
            w = jnp.zeros((n_keys, LANES), F32)
            for h in range(p_heads):
                thr = thr_rows[h][ii:ii + 1, :]
                sc = sc_rows[h][ii:ii + 1, :]
                w = w + jnp.where(st_ref[2 * h + 1, :, cs] >= thr, e1_ref[h, :, cs] * sc, 0.0)
            s = st[ii * n_keys:(ii + 1) * n_keys, cs]
            act = 0.5 * s * (1.0 + lax.erf(s * inv_sqrt2))
            p_ref[ii * n_keys:(ii + 1) * n_keys, cs] = (act * w).astype(BF16)
    o_ref[...] += _dot(vt_ref[...], p_ref[...])


def _peer_dense(h1t, st, stats, u_b, vt_b, *, tt, et):
    d, n = h1t.shape
    n2, n_keys, _ = st.shape
    p_heads = n2 // 2
    ne = u_b.shape[0]
    assert tt % LANES == 0 and et % (8 * n_keys) == 0 and ne % et == 0
    kern = functools.partial(_peer_dense_kernel, p_heads=p_heads, n_keys=n_keys, et=et, tt=tt)
    return pl.pallas_call(
        kern,
        grid=(n // tt, ne // et),
        in_specs=[pl.BlockSpec((d, tt), lambda i, e: (0, i)),
                  pl.BlockSpec((n2, n_keys, tt), lambda i, e: (0, 0, i)),
                  pl.BlockSpec((4 * p_heads, tt), lambda i, e: (0, i)),
                  pl.BlockSpec((et, d), lambda i, e: (e, 0)),
                  pl.BlockSpec((d, et), lambda i, e: (0, e))],
        out_specs=pl.BlockSpec((d, tt), lambda i, e: (0, i)),
        out_shape=jax.ShapeDtypeStruct((d, n), F32),
        scratch_shapes=[pltpu.VMEM((p_heads, n_keys, tt), F32), pltpu.VMEM((p_heads, n_keys, tt), F32),
                        pltpu.VMEM((p_heads, n_keys, tt), F32), pltpu.VMEM((et, tt), BF16)],
        compiler_params=_cparams(("parallel", "arbitrary")),
        name="peer_dense",
    )(h1t, st, stats, u_b, vt_b)


def _finish_b_kernel(h1_ref, pt_ref, p_ref, g_ref, b_ref, wg_ref, bg_ref, wp_ref, y_ref, *, alpha):
    h2 = _layernorm(alpha * h1_ref[...] + pt_ref[...].T, g_ref[...], b_ref[...])
    gate = jax.nn.sigmoid(_dot(h2.astype(BF16), wg_ref[...]) + bg_ref[...])
    y_ref[...] = h2 + _dot(p_ref[...].astype(BF16), wp_ref[...]) * gate


def _finish_b(h1, peer_t, p, g, b, wg, bg, wp, *, alpha, tm):
    n, d = h1.shape
    row = lambda i: (i, 0)
    return pl.pallas_call(
        functools.partial(_finish_b_kernel, alpha=alpha),
        grid=(n // tm,),
        in_specs=[pl.BlockSpec((tm, d), row), pl.BlockSpec((d, tm), lambda i: (0, i)),
                  pl.BlockSpec((tm, p.shape[1]), row), _full(g.shape), _full(b.shape), _full(wg.shape),
                  _full(bg.shape), _full(wp.shape)],
        out_specs=pl.BlockSpec((tm, d), row),
        out_shape=jax.ShapeDtypeStruct((n, d), F32),
        compiler_params=_cparams(("parallel",)),
        name="finish_b",
    )(h1, peer_t, p, g, b, wg, bg, wp)


def _rope_table(pos, rope, nope, scale):
    half = rope // 2
    inv = ROPE_THETA ** (-jnp.arange(half, dtype=F32) / half)
    ang = pos.astype(F32)[:, None] * inv[None, :]
    cos, sin = jnp.cos(ang), jnp.sin(ang)
    n = pos.shape[0]
    zeros = lambda w: jnp.zeros((n, w), F32)
    cq = jnp.concatenate([jnp.ones((n, nope), F32), cos, cos, zeros(LANES - nope - rope)], axis=1) * scale
    sq = jnp.concatenate([zeros(nope), -sin, sin, zeros(LANES - nope - rope)], axis=1) * scale
    ck = jnp.concatenate([cos, cos, zeros(LANES - rope)], axis=1)
    sk = jnp.concatenate([-sin, sin, zeros(LANES - rope)], axis=1)
    return jnp.concatenate([cq, sq, ck, sk], axis=1)


def _swap_halves(w):
    half = w.shape[-1] // 2
    return jnp.concatenate([w[..., half:], w[..., :half]], axis=-1)


def _prep_layer(w_in, w_uq, w_uk, w_uv, dims):
    heads, nope, rope, vdim = dims["heads"], dims["nope"], dims["rope"], dims["vdim"]
    q_rank, kv_rank, sb_width = dims["q_rank"], dims["kv_rank"], dims["sb_width"]
    d = w_in.shape[0]
    cuts = np.cumsum([q_rank, kv_rank, rope, sb_width, sb_width])
    cq, ckv, kr, sq, sk, sv = jnp.split(w_in, [int(c) for c in cuts], axis=1)
    padl = lambda w: jnp.pad(w, ((0, 0), (0, LANES - w.shape[1])))
    w1 = jnp.concatenate([cq, ckv, sq, sk, sv, padl(kr), padl(_swap_halves(kr))], axis=1).astype(BF16)

    hd = nope + rope
    wq3 = w_uq.reshape(q_rank, heads, hd)
    pad3 = lambda w: jnp.pad(w, ((0, 0), (0, 0), (0, LANES - w.shape[2])))
    main = pad3(wq3)
    swap = pad3(jnp.concatenate([jnp.zeros((q_rank, heads, nope), F32), _swap_halves(wq3[..., nope:])], axis=2))
    wq = jnp.concatenate([main.reshape(q_rank, -1), swap.reshape(q_rank, -1)], axis=1).astype(BF16)

    wuk_pad = pad3(w_uk).reshape(kv_rank, heads * LANES)
    wkv = jnp.concatenate([wuk_pad, w_uv.reshape(kv_rank, heads * vdim)], axis=1).astype(BF16)

    e = np.zeros((LANES, heads * LANES), np.float32)
    sel = np.zeros((heads * LANES, LANES), np.float32)
    for h in range(heads):
        for r in range(rope):
            e[r, h * LANES + nope + r] = 1.0
            sel[h * LANES + nope + r, r] = 1.0
    return dict(w1=w1, wq=wq, wkv=wkv, emat=jnp.asarray(e, BF16), sel=jnp.asarray(sel, BF16),
                wuk_pad=wuk_pad.astype(BF16), wuv2=w_uv.reshape(kv_rank, heads * vdim).astype(BF16))


def _pick_tile(n, pref):
    t = pref
    while n % t:
        t //= 2
    return t


def kernel(x_prompt, x_sample, cache_mla_latent, cache_mla_krope, cache_sb_k, cache_sb_v, page_table,
           p_prompt, p_sample, w_in, g_q_norm, g_kv_norm, w_uq, w_uk, w_uv, w_out, ln1_g, ln1_b,
           peer_w_q, peer_sub_keys, peer_u, peer_v, ln2_g, ln2_b, ple_w_proj, ple_w_gate, ple_b_gate):
    depth = w_in.shape[0]
    batch, seq, d = x_prompt.shape
    dec_batch, t_new, _ = x_sample.shape
    n_pool, page = cache_mla_latent.shape[1:3]
    kv_rank, heads, nope = w_uk.shape[1:]
    vdim = w_uv.shape[3]
    rope = cache_mla_krope.shape[3]
    sb_heads, sb_dim = cache_sb_k.shape[3:]
    dims = dict(heads=heads, nope=nope, rope=rope, vdim=vdim, q_rank=w_uq.shape[1], kv_rank=kv_rank,
                sb_heads=sb_heads, sb_dim=sb_dim, sb_width=sb_heads * sb_dim,
                sb_scale=1.0 / math.sqrt(sb_dim))
    assert nope + rope <= LANES and rope % 2 == 0
    mla_scale = 1.0 / math.sqrt(nope + rope)
    alpha = (2 * depth) ** 0.25
    past_len = page_table.shape[1] * page
    n_p, n_s = batch * seq, dec_batch * t_new

    tab_p = _rope_table(jnp.arange(seq), rope, nope, mla_scale)
    tab_s = jnp.tile(_rope_table(past_len + jnp.arange(t_new), rope, nope, mla_scale), (dec_batch, 1))

    hp = x_prompt.reshape(n_p, d)
    hs = x_sample.reshape(n_s, d)
    outs_p, outs_s = [], []
    for i in range(depth):
        lw = _prep_layer(w_in[i], w_uq[i], w_uk[i], w_uv[i], dims)
        gq, gkv = g_q_norm[i][None, :], g_kv_norm[i][None, :]
        wout_b = w_out[i].astype(BF16)
        wpq_b = peer_w_q[i].astype(BF16)
        keys_b = peer_sub_keys[i].reshape(-1, *peer_sub_keys.shape[3:]).astype(BF16)
        u_b = peer_u[i].astype(BF16)
        vt_b = peer_v[i].T.astype(BF16)
        wg_b = ple_w_gate[i].astype(BF16)
        wp_b = ple_w_proj[i].astype(BF16)
        row = lambda v: v[None, :]
        caches = (cache_mla_latent[i], cache_mla_krope[i],
                  cache_sb_k[i].reshape(n_pool, page, -1), cache_sb_v[i].reshape(n_pool, page, -1))

        def finish(h, mix, p):
            n = h.shape[0]
            tm = _pick_tile(n, 256)
            h1, h1t, st, stats = _finish_a(h, mix, wout_b, row(ln1_g[i]), row(ln1_b[i]), wpq_b, keys_b,
                                           alpha=alpha, tm=tm)
            peer_t = _peer_dense(h1t, st, stats, u_b, vt_b, tt=_pick_tile(n, 512), et=1024)
            return _finish_b(h1, peer_t, p.reshape(n, -1), row(ln2_g[i]), row(ln2_b[i]), wg_b,
                             row(ple_b_gate[i]), wp_b, alpha=alpha, tm=tm)

        lat, kr, sk, sv, qpad, _, _, sqb, _, _ = _inproj(
            hs, tab_s, lw["w1"], gq, gkv, lw["wq"], lw["wkv"], lw["emat"], dims=dims, tm=_pick_tile(n_s, 256))
        mix = _decode_attn(qpad, sqb, lat, kr, sk, sv, lw["wuk_pad"], lw["sel"], lw["wuv2"], caches,
                           page_table, dec_batch=dec_batch, t_new=t_new, dims=dims,
                           group=math.gcd(8, page_table.shape[1] // 2))
        outs_s.append((lat, kr, sk, sv))
        hs = finish(hs, mix, p_sample[i])

        lat, kr, sk, sv, qpad, kpad, vb, sqb, skb, svb = _inproj(
            hp, tab_p, lw["w1"], gq, gkv, lw["wq"], lw["wkv"], lw["emat"], dims=dims, tm=_pick_tile(seq, 256))
        mix = _prompt_attn(qpad, kpad, vb, sqb, skb, svb, batch=batch, seq=seq, dims=dims,
                           blk=_pick_tile(seq, 256))
        outs_p.append((lat, kr, sk, sv))
        hp = finish(hp, mix, p_prompt[i])

    def stack(outs, k, shape):
        return jnp.stack([o[k].reshape(shape) for o in outs])

    return (hp.reshape(batch, seq, d), hs.reshape(dec_batch, t_new, d),
            stack(outs_p, 0, (batch, seq, kv_rank)), stack(outs_p, 1, (batch, seq, rope)),
            stack(outs_p, 2, (batch, seq, sb_heads, sb_dim)), stack(outs_p, 3, (batch, seq, sb_heads, sb_dim)),
            stack(outs_s, 0, (dec_batch, t_new, kv_rank)), stack(outs_s, 1, (dec_batch, t_new, rope)),
            stack(outs_s, 2, (dec_batch, t_new, sb_heads, sb_dim)),
            stack(outs_s, 3, (dec_batch, t_new, sb_heads, sb_dim)))
```

```python
import functools
import math

import jax
import jax.numpy as jnp
import numpy as np
from jax import lax
from jax.experimental import pallas as pl
from jax.experimental.pallas import tpu as pltpu

F32 = jnp.float32
BF16 = jnp.bfloat16

ROPE_THETA = 10000.0
EPS = 1e-6
PEER_TOPK = 16
LANES = 128
VMEM_LIMIT = 56 * 1024 * 1024
NEG_INF = float("-inf")


def _cparams(sem):
    return pltpu.CompilerParams(dimension_semantics=sem, vmem_limit_bytes=VMEM_LIMIT)


def _full(shape):
    n = len(shape)
    return pl.BlockSpec(shape, lambda *_: (0,) * n)


def _dot(a, b):
    return jnp.dot(a, b, preferred_element_type=F32)


def _dot_nt(a, b):
    return lax.dot_general(a, b, (((1,), (1,)), ((), ())), preferred_element_type=F32)


def _layernorm(x, g, b):
    mu = jnp.mean(x, axis=-1, keepdims=True)
    xc = x - mu
    var = jnp.mean(xc * xc, axis=-1, keepdims=True)
    return xc * lax.rsqrt(var + EPS) * g + b


def _rmsnorm(x, g):
    return x * lax.rsqrt(jnp.mean(x * x, axis=-1, keepdims=True) + EPS) * g


def _neg_softplus(z):
    return -(jnp.maximum(z, 0.0) + jnp.log(1.0 + jnp.exp(-jnp.abs(z))))


def _split_hi_lo(x):
    hi = x.astype(BF16)
    lo = (x - hi.astype(F32)).astype(BF16)
    return hi, lo


def _inproj_kernel(x_ref, tab_ref, w1_ref, gq_ref, gkv_ref, wq_ref, wkv_ref, e_ref,
                   lat_ref, kr_ref, sk_ref, sv_ref, qpad_ref, kpad_ref, vb_ref, sqb_ref, skb_ref,
                   svb_ref, *, q_rank, kv_rank, sb_width, rope, heads, sb_scale):
    xb = x_ref[...].astype(BF16)
    proj = _dot(xb, w1_ref[...])
    c1 = q_rank
    c2 = c1 + kv_rank
    c3 = c2 + sb_width
    c4 = c3 + sb_width
    c5 = c4 + sb_width
    cq = proj[:, :c1]
    ckv = proj[:, c1:c2]
    sq = proj[:, c2:c3]
    sk = proj[:, c3:c4]
    sv = proj[:, c4:c5]
    kr_a = proj[:, c5:c5 + LANES]
    kr_b = proj[:, c5 + LANES:c5 + 2 * LANES]

    tab = tab_ref[...]
    cq_t = tab[:, 0:LANES]
    sq_t = tab[:, LANES:2 * LANES]
    ck_t = tab[:, 2 * LANES:3 * LANES]
    sk_t = tab[:, 3 * LANES:4 * LANES]

    cqn = _rmsnorm(cq, gq_ref[...]).astype(BF16)
    qq = _dot(cqn, wq_ref[...])
    hw = heads * LANES
    for h in range(heads):
        sl = slice(h * LANES, (h + 1) * LANES)
        sl2 = slice(hw + h * LANES, hw + (h + 1) * LANES)
        qpad_ref[:, sl] = (qq[:, sl] * cq_t + qq[:, sl2] * sq_t).astype(BF16)

    lat = _rmsnorm(ckv, gkv_ref[...])
    lat_ref[...] = lat
    kv = _dot(lat.astype(BF16), wkv_ref[...])
    krot = kr_a * ck_t + kr_b * sk_t
    kr_ref[...] = krot[:, :rope]
    kpad_ref[...] = (kv[:, :hw] + _dot(krot.astype(BF16), e_ref[...])).astype(BF16)
    vb_ref[...] = kv[:, hw:].astype(BF16)
    sqb_ref[...] = (sq * sb_scale).astype(BF16)
    skb_ref[...] = sk.astype(BF16)
    svb_ref[...] = sv.astype(BF16)
    sk_ref[...] = sk
    sv_ref[...] = sv


def _inproj(x, tab, w1, gq, gkv, wq, wkv, emat, *, dims, tm):
    n, d = x.shape
    heads, nope, rope, vdim = dims["heads"], dims["nope"], dims["rope"], dims["vdim"]
    q_rank, kv_rank, sb_width = dims["q_rank"], dims["kv_rank"], dims["sb_width"]
    hw = heads * LANES
    period = tab.shape[0] // tm
    row = lambda i: (i, 0)
    out_shapes = (
        jax.ShapeDtypeStruct((n, kv_rank), F32),
        jax.ShapeDtypeStruct((n, rope), F32),
        jax.ShapeDtypeStruct((n, sb_width), F32),
        jax.ShapeDtypeStruct((n, sb_width), F32),
        jax.ShapeDtypeStruct((n, hw), BF16),
        jax.ShapeDtypeStruct((n, hw), BF16),
        jax.ShapeDtypeStruct((n, heads * vdim), BF16),
        jax.ShapeDtypeStruct((n, sb_width), BF16),
        jax.ShapeDtypeStruct((n, sb_width), BF16),
        jax.ShapeDtypeStruct((n, sb_width), BF16),
    )
    kern = functools.partial(_inproj_kernel, q_rank=q_rank, kv_rank=kv_rank, sb_width=sb_width,
                             rope=rope, heads=heads, sb_scale=dims["sb_scale"])
    return pl.pallas_call(
        kern,
        grid=(n // tm,),
        in_specs=[
            pl.BlockSpec((tm, d), row),
            pl.BlockSpec((tm, 4 * LANES), lambda i: (i % period, 0)),
            _full(w1.shape), _full(gq.shape), _full(gkv.shape), _full(wq.shape), _full(wkv.shape),
            _full(emat.shape),
        ],
        out_specs=tuple(pl.BlockSpec((tm, s.shape[1]), row) for s in out_shapes),
        out_shape=out_shapes,
        compiler_params=_cparams(("parallel",)),
        name="inproj",
    )(x, tab, w1, gq, gkv, wq, wkv, emat)


def _prompt_attn_kernel(q_ref, k_ref, v_ref, sq_ref, sk_ref, sv_ref, tri_ref, o_ref, *,
                        blk, heads, vdim, sb_heads, sb_dim, hgroup):
    i = pl.program_id(1)
    row = lax.broadcasted_iota(jnp.int32, (blk, blk), 0)
    col = lax.broadcasted_iota(jnp.int32, (blk, blk), 1)
    tri = tri_ref[...]
    mla_w = heads * vdim

    def mla_step(h, off, carry, masked):
        m, l, acc = carry
        qh = q_ref[:, h * LANES:(h + 1) * LANES]
        kh = k_ref[pl.ds(off, blk), h * LANES:(h + 1) * LANES]
        vh = v_ref[pl.ds(off, blk), h * vdim:(h + 1) * vdim]
        s = _dot_nt(qh, kh)
        if masked:
            s = jnp.where(row >= col, s, NEG_INF)
        m_new = jnp.maximum(m, jnp.max(s, axis=-1, keepdims=True))
        corr = jnp.exp(m - m_new)
        p = jnp.exp(s - m_new)
        l = l * corr + jnp.sum(p, axis=-1, keepdims=True)
        acc = acc * corr + _dot(p.astype(BF16), vh)
        return m_new, l, acc

    def sb_step(h, off, carry, masked):
        log_carry, acc = carry
        sqh = sq_ref[:, h * sb_dim:(h + 1) * sb_dim]
        kh = sk_ref[pl.ds(off, blk), h * sb_dim:(h + 1) * sb_dim]
        vh = sv_ref[pl.ds(off, blk), h * sb_dim:(h + 1) * sb_dim]
        z = _dot_nt(sqh, kh)
        lneg = _neg_softplus(z)
        lpos = z + lneg
        if masked:
            lneg = jnp.where(row > col, lneg, 0.0)
        hi, lo = _split_hi_lo(lneg)
        suffix = _dot(hi, tri) + _dot(lo, tri)
        w = jnp.exp(lpos + suffix + log_carry)
        if masked:
            w = jnp.where(row > col, w, 0.0)
        acc = acc + _dot(w.astype(BF16), vh)
        log_carry = log_carry + jnp.sum(lneg, axis=-1, keepdims=True)
        return log_carry, acc

    mla_init = (jnp.full((blk, 1), NEG_INF, F32), jnp.zeros((blk, 1), F32), jnp.zeros((blk, vdim), F32))
    sb_init = (jnp.zeros((blk, 1), F32), jnp.zeros((blk, sb_dim), F32))
    for h0 in range(0, heads, hgroup):
        hs = tuple(range(h0, h0 + hgroup))

        def step(j, carry, masked, hs=hs):
            off = pl.multiple_of((i - j) * blk, blk)
            mla_c, sb_c = carry
            return (tuple(mla_step(h, off, c, masked) for h, c in zip(hs, mla_c)),
                    tuple(sb_step(h, off, c, masked) for h, c in zip(hs, sb_c)))

        carry = step(0, ((mla_init,) * hgroup, (sb_init,) * hgroup), True)
        mla_c, sb_c = lax.fori_loop(1, i + 1, functools.partial(step, masked=False), carry)
        for h, (_, l, acc) in zip(hs, mla_c):
            o_ref[:, h * vdim:(h + 1) * vdim] = (acc / l).astype(o_ref.dtype)
        for h, (_, acc) in zip(hs, sb_c):
            o_ref[:, mla_w + h * sb_dim:mla_w + (h + 1) * sb_dim] = acc.astype(o_ref.dtype)


def _strict_lower(n):
    r = np.arange(n)
    return jnp.asarray((r[:, None] > r[None, :]).astype(np.float32), dtype=BF16)


def _prompt_attn(qpad, kpad, vb, sqb, skb, svb, *, batch, seq, dims, blk):
    heads, vdim = dims["heads"], dims["vdim"]
    sb_heads, sb_dim = dims["sb_heads"], dims["sb_dim"]
    nq = seq // blk
    hw = heads * LANES
    mix_w = heads * vdim + sb_heads * sb_dim
    qrow = lambda b, i: (b * nq + i, 0)
    per_b = lambda b, i: (b, 0, 0)
    assert heads == sb_heads
    kern = functools.partial(_prompt_attn_kernel, blk=blk, heads=heads, vdim=vdim,
                             sb_heads=sb_heads, sb_dim=sb_dim, hgroup=math.gcd(heads, 4))
    return pl.pallas_call(
        kern,
        grid=(batch, nq),
        in_specs=[
            pl.BlockSpec((blk, hw), qrow),
            pl.BlockSpec((None, seq, hw), per_b),
            pl.BlockSpec((None, seq, heads * vdim), per_b),
            pl.BlockSpec((blk, sb_heads * sb_dim), qrow),
            pl.BlockSpec((None, seq, sb_heads * sb_dim), per_b),
            pl.BlockSpec((None, seq, sb_heads * sb_dim), per_b),
            _full((blk, blk)),
        ],
        out_specs=pl.BlockSpec((blk, mix_w), qrow),
        out_shape=jax.ShapeDtypeStruct((batch * seq, mix_w), BF16),
        compiler_params=_cparams(("parallel", "arbitrary")),
        name="prompt_attn",
    )(qpad, kpad.reshape(batch, seq, hw), vb.reshape(batch, seq, -1), sqb,
      skb.reshape(batch, seq, -1), svb.reshape(batch, seq, -1), _strict_lower(blk))


def _decode_kernel(pt_ref, qpad_ref, sqb_ref, nlat_ref, nkr_ref, nk_ref, nv_ref,
                   wuk_ref, sel_ref, wuv_ref, tri_ref,
                   clat_hbm, ckr_hbm, ck_hbm, cv_hbm,
                   o_ref,
                   lat_buf, kr_buf, k_buf, v_buf, sems, *,
                   layer, t_new, heads, vdim, sb_heads, sb_dim, page, group, n_pages, kv_rank):
    b = pl.program_id(0)
    nb = pl.num_programs(0)
    n_groups = n_pages // group
    rows = t_new * heads
    chunk = LANES

    def copies(bb, g, slot):
        out = []
        base = n_pages - (g + 1) * group
        for p in range(group):
            phys = pt_ref[bb, base + p]
            dst = pl.ds(p * page, page)
            out.append(pltpu.make_async_copy(clat_hbm.at[layer, phys], lat_buf.at[slot, dst], sems.at[0, slot]))
            out.append(pltpu.make_async_copy(ckr_hbm.at[layer, phys], kr_buf.at[slot, :, dst], sems.at[1, slot]))
            out.append(pltpu.make_async_copy(ck_hbm.at[layer, phys], k_buf.at[slot, :, dst], sems.at[2, slot]))
            out.append(pltpu.make_async_copy(cv_hbm.at[layer, phys], v_buf.at[slot, :, dst], sems.at[3, slot]))
        return out

    @pl.when(b == 0)
    def _():
        for c in copies(0, 0, 0):
            c.start()

    def per_head_rows(x, width):
        x = x.astype(F32)
        full = jnp.concatenate(
            [jnp.broadcast_to(x[t:t + 1, :], (heads, x.shape[1])) for t in range(t_new)], axis=0)
        r = lax.broadcasted_iota(jnp.int32, full.shape, 0)
        c = lax.broadcasted_iota(jnp.int32, full.shape, 1)
        return jnp.where((c // width) == (r % heads), full, 0.0).astype(BF16)

    rope = kr_buf.shape[1]
    q_bd = per_head_rows(qpad_ref[...], LANES)
    q_abs = _dot_nt(q_bd, wuk_ref[...]).astype(BF16)
    q_rope = _dot(q_bd, sel_ref[...])[:, :rope].astype(BF16)
    sq_bd = per_head_rows(sqb_ref[...], sb_dim)
    tri = tri_ref[...]

    def process(lat, kr, kk, vv, carry, mask_mla=None, mask_sb=None):
        m, l, acc, o_sb, ls = carry
        nk = lat.shape[0]
        s = _dot_nt(q_abs, lat) + _dot(q_rope, kr)
        if mask_mla is not None:
            s = jnp.where(mask_mla, s, NEG_INF)
        m_new = jnp.maximum(m, jnp.max(s, axis=-1, keepdims=True))
        corr = jnp.exp(m - m_new)
        p = jnp.exp(s - m_new)
        l = l * corr + jnp.sum(p, axis=-1, keepdims=True)
        acc = acc * corr + _dot(p.astype(BF16), lat)

        z = _dot(sq_bd, kk)
        lneg = _neg_softplus(z)
        lpos = z + lneg
        if mask_sb is not None:
            lneg = jnp.where(mask_sb, lneg, 0.0)
        nch = nk // chunk
        hi, lo = _split_hi_lo(lneg)
        stacked = jnp.concatenate([hi[:, c * chunk:(c + 1) * chunk] for c in range(nch)]
                                  + [lo[:, c * chunk:(c + 1) * chunk] for c in range(nch)], axis=0)
        suf = _dot(stacked, tri)
        pieces = [None] * nch
        run = ls
        for c in range(nch - 1, -1, -1):
            within = suf[c * rows:(c + 1) * rows] + suf[(nch + c) * rows:(nch + c + 1) * rows]
            pieces[c] = within + run
            run = run + jnp.sum(lneg[:, c * chunk:(c + 1) * chunk], axis=-1, keepdims=True)
        suffix = pieces[0] if nch == 1 else jnp.concatenate(pieces, axis=1)
        w = jnp.exp(lpos + suffix)
        if mask_sb is not None:
            w = jnp.where(mask_sb, w, 0.0)
        o_sb = o_sb + _dot_nt(w.astype(BF16), vv)
        return m_new, l, acc, o_sb, run

    r_i = lax.broadcasted_iota(jnp.int32, (rows, chunk), 0) // heads
    c_i = lax.broadcasted_iota(jnp.int32, (rows, chunk), 1)
    valid = c_i < t_new
    init = (jnp.full((rows, 1), NEG_INF, F32), jnp.zeros((rows, 1), F32), jnp.zeros((rows, kv_rank), F32),
            jnp.zeros((rows, sb_heads * sb_dim), F32), jnp.zeros((rows, 1), F32))
    carry = process(nlat_ref[...].astype(BF16), nkr_ref[...].astype(BF16), nk_ref[...].astype(BF16),
                    nv_ref[...].astype(BF16), init,
                    mask_mla=valid & (r_i >= c_i), mask_sb=valid & (r_i > c_i))

    def group_step(g, carry):
        slot = g % 2
        nxt = 1 - slot

        @pl.when(g + 1 < n_groups)
        def _():
            for c in copies(b, g + 1, nxt):
                c.start()

        @pl.when((g + 1 == n_groups) & (b + 1 < nb))
        def _():
            for c in copies(b + 1, 0, nxt):
                c.start()

        for c in copies(b, g, slot):
            c.wait()
        return process(lat_buf[slot].astype(BF16), kr_buf[slot].astype(BF16),
                       k_buf[slot].astype(BF16), v_buf[slot].astype(BF16), carry)

    m, l, acc, o_sb, _ = lax.fori_loop(0, n_groups, group_step, carry)

    o_all = _dot((acc / l).astype(BF16), wuv_ref[...])

    def store_own_head(x, width, lane0):
        r = lax.broadcasted_iota(jnp.int32, x.shape, 0)
        c = lax.broadcasted_iota(jnp.int32, x.shape, 1)
        xm = jnp.where((c // width) == (r % heads), x, 0.0)
        for t in range(t_new):
            o_ref[t:t + 1, lane0:lane0 + heads * width] = jnp.sum(
                xm[t * heads:(t + 1) * heads], axis=0, keepdims=True)

    store_own_head(o_all, vdim, 0)
    store_own_head(o_sb, sb_dim, heads * vdim)


def _decode_attn(qpad, sqb, lat, kr, sk, sv, wuk_pad, sel, wuv2, caches, page_table, *,
                 dec_batch, t_new, dims, group, layer):
    heads, vdim = dims["heads"], dims["vdim"]
    sb_heads, sb_dim = dims["sb_heads"], dims["sb_dim"]
    kv_rank, rope = dims["kv_rank"], dims["rope"]
    sb_width = sb_heads * sb_dim
    c_lat, c_kr, c_k, c_v = caches
    page = c_lat.shape[2]
    n_pages = page_table.shape[1]
    assert heads == sb_heads and n_pages % group == 0 and (n_pages // group) % 2 == 0
    assert page == LANES and t_new <= LANES
    hw = heads * LANES
    mix_w = heads * vdim + sb_width

    def new_keys(x, transpose):
        x = jnp.pad(x.reshape(dec_batch, t_new, -1), ((0, 0), (0, page - t_new), (0, 0)))
        return jnp.transpose(x, (0, 2, 1)) if transpose else x

    nlat = new_keys(lat, False)
    nkr = new_keys(kr, True)
    nk = new_keys(sk, True)
    nv = new_keys(sv, True)
    per_b = lambda b, pt: (b, 0, 0)
    const2 = lambda b, pt: (0, 0)
    kern = functools.partial(_decode_kernel, layer=layer, t_new=t_new, heads=heads, vdim=vdim,
                             sb_heads=sb_heads, sb_dim=sb_dim, page=page, group=group, n_pages=n_pages,
                             kv_rank=kv_rank)
    gk = group * page
    grid_spec = pltpu.PrefetchScalarGridSpec(
        num_scalar_prefetch=1,
        grid=(dec_batch,),
        in_specs=[
            pl.BlockSpec((None, t_new, hw), per_b),
            pl.BlockSpec((None, t_new, sb_width), per_b),
            pl.BlockSpec((None, page, kv_rank), per_b),
            pl.BlockSpec((None, rope, page), per_b),
            pl.BlockSpec((None, sb_width, page), per_b),
            pl.BlockSpec((None, sb_width, page), per_b),
            pl.BlockSpec(wuk_pad.shape, const2),
            pl.BlockSpec(sel.shape, const2),
            pl.BlockSpec(wuv2.shape, const2),
            pl.BlockSpec((LANES, LANES), const2),
            pl.BlockSpec(memory_space=pl.ANY),
            pl.BlockSpec(memory_space=pl.ANY),
            pl.BlockSpec(memory_space=pl.ANY),
            pl.BlockSpec(memory_space=pl.ANY),
        ],
        out_specs=pl.BlockSpec((None, t_new, mix_w), per_b),
        scratch_shapes=[
            pltpu.VMEM((2, gk, kv_rank), F32),
            pltpu.VMEM((2, rope, gk), F32),
            pltpu.VMEM((2, sb_width, gk), F32),
            pltpu.VMEM((2, sb_width, gk), F32),
            pltpu.SemaphoreType.DMA((4, 2)),
        ],
    )
    out = pl.pallas_call(
        kern,
        grid_spec=grid_spec,
        out_shape=jax.ShapeDtypeStruct((dec_batch, t_new, mix_w), F32),
        compiler_params=_cparams(("arbitrary",)),
        name="decode_attn",
    )(page_table, qpad.reshape(dec_batch, t_new, hw), sqb.reshape(dec_batch, t_new, sb_width),
      nlat, nkr, nk, nv, wuk_pad, sel, wuv2, _strict_lower(LANES), c_lat, c_kr, c_k, c_v)
    return out.reshape(dec_batch * t_new, mix_w)


def _finish_a_kernel(x_ref, mix_ref, wout_ref, g_ref, b_ref, wpq_ref, keys_ref,
                     h1_ref, h1t_ref, st_ref, stats_ref, v0_ref, v1_ref, *, alpha, p_heads, n_keys, topk):
    h1 = _layernorm(alpha * x_ref[...] + _dot(mix_ref[...].astype(BF16), wout_ref[...]),
                    g_ref[...], b_ref[...])
    h1_ref[...] = h1
    h1t_ref[...] = h1.T.astype(BF16)
    qp = _dot(h1.astype(BF16), wpq_ref[...]).astype(BF16)
    qhalf = qp.shape[1] // (2 * p_heads)

    def top_vals(s, dst_ref):
        cur = s
        for r in range(topk):
            mx = jnp.max(cur, axis=0, keepdims=True)
            dst_ref[r:r + 1, :] = mx
            cur = jnp.where(cur == mx, NEG_INF, cur)

    for h in range(p_heads):
        s0 = _dot_nt(keys_ref[2 * h], qp[:, (2 * h) * qhalf:(2 * h + 1) * qhalf])
        s1 = _dot_nt(keys_ref[2 * h + 1], qp[:, (2 * h + 1) * qhalf:(2 * h + 2) * qhalf])
        st_ref[2 * h] = s0
        st_ref[2 * h + 1] = s1
        top_vals(s0, v0_ref)
        top_vals(s1, v1_ref)
        v0 = v0_ref[...]
        v1 = v1_ref[...]
        half = topk // 2
        parts = [v0[0:1, :] + v1]
        for a in range(1, half):
            parts.append(v0[a:a + 1, :] + v1[0:half, :])
        parts.append(v0[half:topk, :] + v1[0:1, :])
        cand = jnp.concatenate(parts, axis=0)
        cur = cand
        tau = None
        for r in range(topk):
            tau = jnp.max(cur, axis=0, keepdims=True)
            cur = jnp.where(cur == tau, NEG_INF, cur)
        m0 = v0[0:1, :]
        m1 = v1[0:1, :]
        zsum = jnp.sum(jnp.where(cand >= tau, jnp.exp(cand - (m0 + m1)), 0.0), axis=0, keepdims=True)
        stats_ref[h:h + 1, :] = tau
        stats_ref[p_heads + h:p_heads + h + 1, :] = m0
        stats_ref[2 * p_heads + h:2 * p_heads + h + 1, :] = m1
        stats_ref[3 * p_heads + h:3 * p_heads + h + 1, :] = 1.0 / zsum


def _finish_a(x, mix, wout, g, b, wpq, keys, *, alpha, tm):
    n, d = x.shape
    n2, n_keys, _ = keys.shape
    p_heads = n2 // 2
    row = lambda i: (i, 0)
    kern = functools.partial(_finish_a_kernel, alpha=alpha, p_heads=p_heads, n_keys=n_keys, topk=PEER_TOPK)
    return pl.pallas_call(
        kern,
        grid=(n // tm,),
        in_specs=[pl.BlockSpec((tm, d), row), pl.BlockSpec((tm, mix.shape[1]), row), _full(wout.shape),
                  _full(g.shape), _full(b.shape), _full(wpq.shape), _full(keys.shape)],
        out_specs=(pl.BlockSpec((tm, d), row),
                   pl.BlockSpec((d, tm), lambda i: (0, i)),
                   pl.BlockSpec((n2, n_keys, tm), lambda i: (0, 0, i)),
                   pl.BlockSpec((4 * p_heads, tm), lambda i: (0, i))),
        out_shape=(jax.ShapeDtypeStruct((n, d), F32),
                   jax.ShapeDtypeStruct((d, n), BF16),
                   jax.ShapeDtypeStruct((n2, n_keys, n), F32),
                   jax.ShapeDtypeStruct((4 * p_heads, n), F32)),
        scratch_shapes=[pltpu.VMEM((PEER_TOPK, tm), F32), pltpu.VMEM((PEER_TOPK, tm), F32)],
        compiler_params=_cparams(("parallel",)),
        name="finish_a",
    )(x, mix, wout, g, b, wpq, keys)


def _peer_dense_kernel(h1t_ref, st_ref, stats_ref, u_ref, vt_ref, o_ref,
                       thr_ref, sc_ref, e1_ref, p_ref, *, p_heads, n_keys, et, tt, sub):
    e = pl.program_id(1)

    @pl.when(e == 0)
    def _():
        o_ref[...] = jnp.zeros_like(o_ref)
        for h in range(p_heads):
            tau = stats_ref[h:h + 1, :]
            m0 = stats_ref[p_heads + h:p_heads + h + 1, :]
            m1 = stats_ref[2 * p_heads + h:2 * p_heads + h + 1, :]
            zinv = stats_ref[3 * p_heads + h:3 * p_heads + h + 1, :]
            s0 = st_ref[2 * h]
            thr_ref[h] = tau - s0
            sc_ref[h] = jnp.exp(s0 - m0) * zinv
            e1_ref[h] = jnp.exp(st_ref[2 * h + 1] - m1)

    per_tile = et // n_keys
    inv_sqrt2 = 1.0 / math.sqrt(2.0)
    i0 = pl.multiple_of(e * per_tile, per_tile)
    h1t = h1t_ref[...]
    out = None

    def pre_act(q):
        return _dot(u_ref[q * sub:(q + 1) * sub, :], h1t)

    n_sub = et // sub
    st_next = pre_act(0)
    for q in range(n_sub):
        qs = slice(q * sub, (q + 1) * sub)
        st = st_next
        if q + 1 < n_sub:
            st_next = pre_act(q + 1)
        for c in range(tt // LANES):
            cs = slice(c * LANES, (c + 1) * LANES)
            thr_rows = [thr_ref[h, pl.ds(i0, per_tile), cs] for h in range(p_heads)]
            sc_rows = [sc_ref[h, pl.ds(i0, per_tile), cs] for h in range(p_heads)]
            for r in range(sub // n_keys):
                ii = q * (sub // n_keys) + r
                w = jnp.zeros((n_keys, LANES), F32)
                for h in range(p_heads):
                    thr = thr_rows[h][ii:ii + 1, :]
                    sc = sc_rows[h][ii:ii + 1, :]
                    w = w + jnp.where(st_ref[2 * h + 1, :, cs] >= thr, e1_ref[h, :, cs] * sc, 0.0)
                s = st[r * n_keys:(r + 1) * n_keys, cs]
                act = 0.5 * s * (1.0 + lax.erf(s * inv_sqrt2))
                p_ref[ii * n_keys:(ii + 1) * n_keys, cs] = (act * w).astype(BF16)
        part = _dot(vt_ref[:, qs], p_ref[qs, :])
        out = part if out is None else out + part
    o_ref[...] += out


def _peer_dense(h1t, st, stats, u_b, vt_b, *, tt, et):
    d, n = h1t.shape
    n2, n_keys, _ = st.shape
    p_heads = n2 // 2
    ne = u_b.shape[0]
    assert tt % LANES == 0 and et % (8 * n_keys) == 0 and ne % et == 0
    kern = functools.partial(_peer_dense_kernel, p_heads=p_heads, n_keys=n_keys, et=et, tt=tt,
                             sub=2 * n_keys)
    return pl.pallas_call(
        kern,
        grid=(n // tt, ne // et),
        in_specs=[pl.BlockSpec((d, tt), lambda i, e: (0, i)),
                  pl.BlockSpec((n2, n_keys, tt), lambda i, e: (0, 0, i)),
                  pl.BlockSpec((4 * p_heads, tt), lambda i, e: (0, i)),
                  pl.BlockSpec((et, d), lambda i, e: (e, 0)),
                  pl.BlockSpec((d, et), lambda i, e: (0, e))],
        out_specs=pl.BlockSpec((d, tt), lambda i, e: (0, i)),
        out_shape=jax.ShapeDtypeStruct((d, n), F32),
        scratch_shapes=[pltpu.VMEM((p_heads, n_keys, tt), F32), pltpu.VMEM((p_heads, n_keys, tt), F32),
                        pltpu.VMEM((p_heads, n_keys, tt), F32), pltpu.VMEM((et, tt), BF16)],
        compiler_params=_cparams(("parallel", "arbitrary")),
        name="peer_dense",
    )(h1t, st, stats, u_b, vt_b)


def _finish_b_kernel(h1_ref, pt_ref, p_ref, g_ref, b_ref, wg_ref, bg_ref, wp_ref, y_ref, *, alpha):
    h2 = _layernorm(alpha * h1_ref[...] + pt_ref[...].T, g_ref[...], b_ref[...])
    gate = jax.nn.sigmoid(_dot(h2.astype(BF16), wg_ref[...]) + bg_ref[...])
    y_ref[...] = h2 + _dot(p_ref[...].astype(BF16), wp_ref[...]) * gate


def _finish_b(h1, peer_t, p, g, b, wg, bg, wp, *, alpha, tm):
    n, d = h1.shape
    row = lambda i: (i, 0)
    return pl.pallas_call(
        functools.partial(_finish_b_kernel, alpha=alpha),
        grid=(n // tm,),
        in_specs=[pl.BlockSpec((tm, d), row), pl.BlockSpec((d, tm), lambda i: (0, i)),
                  pl.BlockSpec((tm, p.shape[1]), row), _full(g.shape), _full(b.shape), _full(wg.shape),
                  _full(bg.shape), _full(wp.shape)],
        out_specs=pl.BlockSpec((tm, d), row),
        out_shape=jax.ShapeDtypeStruct((n, d), F32),
        compiler_params=_cparams(("parallel",)),
        name="finish_b",
    )(h1, peer_t, p, g, b, wg, bg, wp)


def _rope_table(pos, rope, nope, scale):
    half = rope // 2
    inv = ROPE_THETA ** (-jnp.arange(half, dtype=F32) / half)
    ang = pos.astype(F32)[:, None] * inv[None, :]
    cos, sin = jnp.cos(ang), jnp.sin(ang)
    n = pos.shape[0]
    zeros = lambda w: jnp.zeros((n, w), F32)
    cq = jnp.concatenate([jnp.ones((n, nope), F32), cos, cos, zeros(LANES - nope - rope)], axis=1) * scale
    sq = jnp.concatenate([zeros(nope), -sin, sin, zeros(LANES - nope - rope)], axis=1) * scale
    ck = jnp.concatenate([cos, cos, zeros(LANES - rope)], axis=1)
    sk = jnp.concatenate([-sin, sin, zeros(LANES - rope)], axis=1)
    return jnp.concatenate([cq, sq, ck, sk], axis=1)


def _swap_halves(w):
    half = w.shape[-1] // 2
    return jnp.concatenate([w[..., half:], w[..., :half]], axis=-1)


def _prep_layer(w_in, w_uq, w_uk, w_uv, dims):
    heads, nope, rope, vdim = dims["heads"], dims["nope"], dims["rope"], dims["vdim"]
    q_rank, kv_rank, sb_width = dims["q_rank"], dims["kv_rank"], dims["sb_width"]
    d = w_in.shape[0]
    cuts = np.cumsum([q_rank, kv_rank, rope, sb_width, sb_width])
    cq, ckv, kr, sq, sk, sv = jnp.split(w_in, [int(c) for c in cuts], axis=1)
    padl = lambda w: jnp.pad(w, ((0, 0), (0, LANES - w.shape[1])))
    w1 = jnp.concatenate([cq, ckv, sq, sk, sv, padl(kr), padl(_swap_halves(kr))], axis=1).astype(BF16)

    hd = nope + rope
    wq3 = w_uq.reshape(q_rank, heads, hd)
    pad3 = lambda w: jnp.pad(w, ((0, 0), (0, 0), (0, LANES - w.shape[2])))
    main = pad3(wq3)
    swap = pad3(jnp.concatenate([jnp.zeros((q_rank, heads, nope), F32), _swap_halves(wq3[..., nope:])], axis=2))
    wq = jnp.concatenate([main.reshape(q_rank, -1), swap.reshape(q_rank, -1)], axis=1).astype(BF16)

    wuk_pad = pad3(w_uk).reshape(kv_rank, heads * LANES)
    wkv = jnp.concatenate([wuk_pad, w_uv.reshape(kv_rank, heads * vdim)], axis=1).astype(BF16)

    e = np.zeros((LANES, heads * LANES), np.float32)
    sel = np.zeros((heads * LANES, LANES), np.float32)
    for h in range(heads):
        for r in range(rope):
            e[r, h * LANES + nope + r] = 1.0
            sel[h * LANES + nope + r, r] = 1.0
    return dict(w1=w1, wq=wq, wkv=wkv, emat=jnp.asarray(e, BF16), sel=jnp.asarray(sel, BF16),
                wuk_pad=wuk_pad.astype(BF16), wuv2=w_uv.reshape(kv_rank, heads * vdim).astype(BF16))


def _pick_tile(n, pref):
    t = pref
    while n % t:
        t //= 2
    return t


def kernel(x_prompt, x_sample, cache_mla_latent, cache_mla_krope, cache_sb_k, cache_sb_v, page_table,
           p_prompt, p_sample, w_in, g_q_norm, g_kv_norm, w_uq, w_uk, w_uv, w_out, ln1_g, ln1_b,
           peer_w_q, peer_sub_keys, peer_u, peer_v, ln2_g, ln2_b, ple_w_proj, ple_w_gate, ple_b_gate):
    depth = w_in.shape[0]
    batch, seq, d = x_prompt.shape
    dec_batch, t_new, _ = x_sample.shape
    n_pool, page = cache_mla_latent.shape[1:3]
    kv_rank, heads, nope = w_uk.shape[1:]
    vdim = w_uv.shape[3]
    rope = cache_mla_krope.shape[3]
    sb_heads, sb_dim = cache_sb_k.shape[3:]
    dims = dict(heads=heads, nope=nope, rope=rope, vdim=vdim, q_rank=w_uq.shape[1], kv_rank=kv_rank,
                sb_heads=sb_heads, sb_dim=sb_dim, sb_width=sb_heads * sb_dim,
                sb_scale=1.0 / math.sqrt(sb_dim))
    assert nope + rope <= LANES and rope % 2 == 0
    mla_scale = 1.0 / math.sqrt(nope + rope)
    alpha = (2 * depth) ** 0.25
    past_len = page_table.shape[1] * page
    n_p, n_s = batch * seq, dec_batch * t_new

    tab_p = _rope_table(jnp.arange(seq), rope, nope, mla_scale)
    tab_s = jnp.tile(_rope_table(past_len + jnp.arange(t_new), rope, nope, mla_scale), (dec_batch, 1))

    keys_last = lambda c: jnp.moveaxis(c, 2, -1).reshape(depth, n_pool, -1, page)
    caches = (cache_mla_latent, keys_last(cache_mla_krope), keys_last(cache_sb_k), keys_last(cache_sb_v))

    hp = x_prompt.reshape(n_p, d)
    hs = x_sample.reshape(n_s, d)
    outs_p, outs_s = [], []
    for i in range(depth):
        lw = _prep_layer(w_in[i], w_uq[i], w_uk[i], w_uv[i], dims)
        gq, gkv = g_q_norm[i][None, :], g_kv_norm[i][None, :]
        wout_b = w_out[i].astype(BF16)
        wpq_b = peer_w_q[i].astype(BF16)
        keys_b = peer_sub_keys[i].reshape(-1, *peer_sub_keys.shape[3:]).astype(BF16)
        u_b = peer_u[i].astype(BF16)
        vt_b = peer_v[i].T.astype(BF16)
        wg_b = ple_w_gate[i].astype(BF16)
        wp_b = ple_w_proj[i].astype(BF16)
        row = lambda v: v[None, :]

        def finish(h, mix, p):
            n = h.shape[0]
            tm = _pick_tile(n, 256)
            h1, h1t, st, stats = _finish_a(h, mix, wout_b, row(ln1_g[i]), row(ln1_b[i]), wpq_b, keys_b,
                                           alpha=alpha, tm=tm)
            peer_t = _peer_dense(h1t, st, stats, u_b, vt_b, tt=_pick_tile(n, 512), et=1024)
            return _finish_b(h1, peer_t, p.reshape(n, -1), row(ln2_g[i]), row(ln2_b[i]), wg_b,
                             row(ple_b_gate[i]), wp_b, alpha=alpha, tm=tm)

        lat, kr, sk, sv, qpad, _, _, sqb, _, _ = _inproj(
            hs, tab_s, lw["w1"], gq, gkv, lw["wq"], lw["wkv"], lw["emat"], dims=dims, tm=_pick_tile(n_s, 256))
        mix = _decode_attn(qpad, sqb, lat, kr, sk, sv, lw["wuk_pad"], lw["sel"], lw["wuv2"], caches,
                           page_table, dec_batch=dec_batch, t_new=t_new, dims=dims,
                           group=math.gcd(8, page_table.shape[1] // 2), layer=i)
        outs_s.append((lat, kr, sk, sv))
        hs = finish(hs, mix, p_sample[i])

        lat, kr, sk, sv, qpad, kpad, vb, sqb, skb, svb = _inproj(
            hp, tab_p, lw["w1"], gq, gkv, lw["wq"], lw["wkv"], lw["emat"], dims=dims, tm=_pick_tile(seq, 256))
        mix = _prompt_attn(qpad, kpad, vb, sqb, skb, svb, batch=batch, seq=seq, dims=dims,
                           blk=_pick_tile(seq, 256))
        outs_p.append((lat, kr, sk, sv))
        hp = finish(hp, mix, p_prompt[i])

    def stack(outs, k, shape):
        return jnp.stack([o[k].reshape(shape) for o in outs])

    return (hp.reshape(batch, seq, d), hs.reshape(dec_batch, t_new, d),
            stack(outs_p, 0, (batch, seq, kv_rank)), stack(outs_p, 1, (batch, seq, rope)),
            stack(outs_p, 2, (batch, seq, sb_heads, sb_dim)), stack(outs_p, 3, (batch, seq, sb_heads, sb_dim)),
            stack(outs_s, 0, (dec_batch, t_new, kv_rank)), stack(outs_s, 1, (dec_batch, t_new, rope)),
            stack(outs_s, 2, (dec_batch, t_new, sb_heads, sb_dim)),
            stack(outs_s, 3, (dec_batch, t_new, sb_heads, sb_dim)))
```

```python
import functools
import math

import jax
import jax.numpy as jnp
import numpy as np
from jax import lax
from jax.experimental import pallas as pl
from jax.experimental.pallas import tpu as pltpu

F32 = jnp.float32
BF16 = jnp.bfloat16

ROPE_THETA = 10000.0
EPS = 1e-6
PEER_TOPK = 16
LANES = 128
VMEM_LIMIT = 56 * 1024 * 1024
NEG_INF = float("-inf")


def _cparams(sem):
    return pltpu.CompilerParams(dimension_semantics=sem, vmem_limit_bytes=VMEM_LIMIT)


def _full(shape):
    n = len(shape)
    return pl.BlockSpec(shape, lambda *_: (0,) * n)


def _dot(a, b):
    return jnp.dot(a, b, preferred_element_type=F32)


def _dot_nt(a, b):
    return lax.dot_general(a, b, (((1,), (1,)), ((), ())), preferred_element_type=F32)


def _layernorm(x, g, b):
    mu = jnp.mean(x, axis=-1, keepdims=True)
    xc = x - mu
    var = jnp.mean(xc * xc, axis=-1, keepdims=True)
    return xc * lax.rsqrt(var + EPS) * g + b


def _rmsnorm(x, g):
    return x * lax.rsqrt(jnp.mean(x * x, axis=-1, keepdims=True) + EPS) * g


def _neg_softplus(z):
    return -(jnp.maximum(z, 0.0) + jnp.log(1.0 + jnp.exp(-jnp.abs(z))))


def _pack_rows(x):
    return pltpu.bitcast(x.astype(BF16), jnp.uint32)


def _split_hi_lo(x):
    hi = x.astype(BF16)
    lo = (x - hi.astype(F32)).astype(BF16)
    return hi, lo


def _inproj_kernel(x_ref, tab_ref, w1_ref, gq_ref, gkv_ref, wq_ref, wkv_ref, e_ref,
                   lat_ref, kr_ref, sk_ref, sv_ref, qpad_ref, kpad_ref, vb_ref, sqb_ref, skb_ref,
                   svb_ref, *, q_rank, kv_rank, sb_width, rope, heads, sb_scale):
    xb = x_ref[...].astype(BF16)
    proj = _dot(xb, w1_ref[...])
    c1 = q_rank
    c2 = c1 + kv_rank
    c3 = c2 + sb_width
    c4 = c3 + sb_width
    c5 = c4 + sb_width
    cq = proj[:, :c1]
    ckv = proj[:, c1:c2]
    sq = proj[:, c2:c3]
    sk = proj[:, c3:c4]
    sv = proj[:, c4:c5]
    kr_a = proj[:, c5:c5 + LANES]
    kr_b = proj[:, c5 + LANES:c5 + 2 * LANES]

    tab = tab_ref[...]
    cq_t = tab[:, 0:LANES]
    sq_t = tab[:, LANES:2 * LANES]
    ck_t = tab[:, 2 * LANES:3 * LANES]
    sk_t = tab[:, 3 * LANES:4 * LANES]

    cqn = _rmsnorm(cq, gq_ref[...]).astype(BF16)
    qq = _dot(cqn, wq_ref[...])
    hw = heads * LANES
    for h in range(heads):
        sl = slice(h * LANES, (h + 1) * LANES)
        sl2 = slice(hw + h * LANES, hw + (h + 1) * LANES)
        qpad_ref[:, sl] = (qq[:, sl] * cq_t + qq[:, sl2] * sq_t).astype(BF16)

    lat = _rmsnorm(ckv, gkv_ref[...])
    lat_ref[...] = lat
    kv = _dot(lat.astype(BF16), wkv_ref[...])
    krot = kr_a * ck_t + kr_b * sk_t
    kr_ref[...] = krot[:, :rope]
    kpad_ref[...] = (kv[:, :hw] + _dot(krot.astype(BF16), e_ref[...])).astype(BF16)
    vb_ref[...] = kv[:, hw:].astype(BF16)
    sqb_ref[...] = (sq * sb_scale).astype(BF16)
    skb_ref[...] = sk.astype(BF16)
    svb_ref[...] = sv.astype(BF16)
    sk_ref[...] = sk
    sv_ref[...] = sv


def _inproj(x, tab, w1, gq, gkv, wq, wkv, emat, *, dims, tm):
    n, d = x.shape
    heads, nope, rope, vdim = dims["heads"], dims["nope"], dims["rope"], dims["vdim"]
    q_rank, kv_rank, sb_width = dims["q_rank"], dims["kv_rank"], dims["sb_width"]
    hw = heads * LANES
    period = tab.shape[0] // tm
    row = lambda i: (i, 0)
    out_shapes = (
        jax.ShapeDtypeStruct((n, kv_rank), F32),
        jax.ShapeDtypeStruct((n, rope), F32),
        jax.ShapeDtypeStruct((n, sb_width), F32),
        jax.ShapeDtypeStruct((n, sb_width), F32),
        jax.ShapeDtypeStruct((n, hw), BF16),
        jax.ShapeDtypeStruct((n, hw), BF16),
        jax.ShapeDtypeStruct((n, heads * vdim), BF16),
        jax.ShapeDtypeStruct((n, sb_width), BF16),
        jax.ShapeDtypeStruct((n, sb_width), BF16),
        jax.ShapeDtypeStruct((n, sb_width), BF16),
    )
    kern = functools.partial(_inproj_kernel, q_rank=q_rank, kv_rank=kv_rank, sb_width=sb_width,
                             rope=rope, heads=heads, sb_scale=dims["sb_scale"])
    return pl.pallas_call(
        kern,
        grid=(n // tm,),
        in_specs=[
            pl.BlockSpec((tm, d), row),
            pl.BlockSpec((tm, 4 * LANES), lambda i: (i % period, 0)),
            _full(w1.shape), _full(gq.shape), _full(gkv.shape), _full(wq.shape), _full(wkv.shape),
            _full(emat.shape),
        ],
        out_specs=tuple(pl.BlockSpec((tm, s.shape[1]), row) for s in out_shapes),
        out_shape=out_shapes,
        compiler_params=_cparams(("parallel",)),
        name="inproj",
    )(x, tab, w1, gq, gkv, wq, wkv, emat)


def _prompt_attn_kernel(q_ref, k_ref, v_ref, sq_ref, sk_ref, sv_ref, tri_ref, o_ref, *,
                        blk, heads, vdim, sb_heads, sb_dim, hgroup):
    i = pl.program_id(1)
    row = lax.broadcasted_iota(jnp.int32, (blk, blk), 0)
    col = lax.broadcasted_iota(jnp.int32, (blk, blk), 1)
    tri = tri_ref[...]
    mla_w = heads * vdim

    def mla_step(h, off, carry, masked):
        m, l, acc = carry
        qh = q_ref[:, h * LANES:(h + 1) * LANES]
        kh = k_ref[pl.ds(off, blk), h * LANES:(h + 1) * LANES]
        vh = v_ref[pl.ds(off, blk), h * vdim:(h + 1) * vdim]
        s = _dot_nt(qh, kh)
        if masked:
            s = jnp.where(row >= col, s, NEG_INF)
        m_new = jnp.maximum(m, jnp.max(s, axis=-1, keepdims=True))
        corr = jnp.exp(m - m_new)
        p = jnp.exp(s - m_new)
        l = l * corr + jnp.sum(p, axis=-1, keepdims=True)
        acc = acc * corr + _dot(p.astype(BF16), vh)
        return m_new, l, acc

    def sb_step(h, off, carry, masked):
        log_carry, acc = carry
        sqh = sq_ref[:, h * sb_dim:(h + 1) * sb_dim]
        kh = sk_ref[pl.ds(off, blk), h * sb_dim:(h + 1) * sb_dim]
        vh = sv_ref[pl.ds(off, blk), h * sb_dim:(h + 1) * sb_dim]
        z = _dot_nt(sqh, kh)
        lneg = _neg_softplus(z)
        lpos = z + lneg
        if masked:
            lneg = jnp.where(row > col, lneg, 0.0)
        hi, lo = _split_hi_lo(lneg)
        suffix = _dot(jnp.concatenate([hi, lo], axis=1), tri)
        w = jnp.exp(lpos + suffix + log_carry)
        if masked:
            w = jnp.where(row > col, w, 0.0)
        acc = acc + _dot(w.astype(BF16), vh)
        log_carry = log_carry + jnp.sum(lneg, axis=-1, keepdims=True)
        return log_carry, acc

    mla_init = (jnp.full((blk, 1), NEG_INF, F32), jnp.zeros((blk, 1), F32), jnp.zeros((blk, vdim), F32))
    sb_init = (jnp.zeros((blk, 1), F32), jnp.zeros((blk, sb_dim), F32))
    for h0 in range(0, heads, hgroup):
        hs = tuple(range(h0, h0 + hgroup))

        def step(j, carry, masked, hs=hs):
            off = pl.multiple_of((i - j) * blk, blk)
            mla_c, sb_c = carry
            return (tuple(mla_step(h, off, c, masked) for h, c in zip(hs, mla_c)),
                    tuple(sb_step(h, off, c, masked) for h, c in zip(hs, sb_c)))

        carry = step(0, ((mla_init,) * hgroup, (sb_init,) * hgroup), True)
        mla_c, sb_c = lax.fori_loop(1, i + 1, functools.partial(step, masked=False), carry)
        for h, (_, l, acc) in zip(hs, mla_c):
            o_ref[:, h * vdim:(h + 1) * vdim] = (acc / l).astype(o_ref.dtype)
        for h, (_, acc) in zip(hs, sb_c):
            o_ref[:, mla_w + h * sb_dim:mla_w + (h + 1) * sb_dim] = acc.astype(o_ref.dtype)


def _strict_lower(n):
    r = np.arange(n)
    return jnp.asarray((r[:, None] > r[None, :]).astype(np.float32), dtype=BF16)


def _prompt_attn(qpad, kpad, vb, sqb, skb, svb, *, batch, seq, dims, blk):
    heads, vdim = dims["heads"], dims["vdim"]
    sb_heads, sb_dim = dims["sb_heads"], dims["sb_dim"]
    nq = seq // blk
    hw = heads * LANES
    mix_w = heads * vdim + sb_heads * sb_dim
    qrow = lambda b, i: (b * nq + i, 0)
    per_b = lambda b, i: (b, 0, 0)
    assert heads == sb_heads
    kern = functools.partial(_prompt_attn_kernel, blk=blk, heads=heads, vdim=vdim,
                             sb_heads=sb_heads, sb_dim=sb_dim, hgroup=math.gcd(heads, 4))
    return pl.pallas_call(
        kern,
        grid=(batch, nq),
        in_specs=[
            pl.BlockSpec((blk, hw), qrow),
            pl.BlockSpec((None, seq, hw), per_b),
            pl.BlockSpec((None, seq, heads * vdim), per_b),
            pl.BlockSpec((blk, sb_heads * sb_dim), qrow),
            pl.BlockSpec((None, seq, sb_heads * sb_dim), per_b),
            pl.BlockSpec((None, seq, sb_heads * sb_dim), per_b),
            _full((2 * blk, blk)),
        ],
        out_specs=pl.BlockSpec((blk, mix_w), qrow),
        out_shape=jax.ShapeDtypeStruct((batch * seq, mix_w), BF16),
        compiler_params=_cparams(("parallel", "arbitrary")),
        name="prompt_attn",
    )(qpad, kpad.reshape(batch, seq, hw), vb.reshape(batch, seq, -1), sqb,
      skb.reshape(batch, seq, -1), svb.reshape(batch, seq, -1),
      jnp.concatenate([_strict_lower(blk)] * 2, axis=0))


def _decode_kernel(pt_ref, qpad_ref, sqb_ref, nlat_ref, nkr_ref, nk_ref, nv_ref,
                   wuk_ref, sel_ref, wuv_ref, tri_ref,
                   clat_hbm, ckr_hbm, ck_hbm, cv_hbm,
                   o_ref,
                   lat_buf, kr_buf, k_buf, v_buf, sems, *,
                   layer, t_new, heads, vdim, sb_heads, sb_dim, page, group, n_pages, kv_rank):
    b = pl.program_id(0)
    nb = pl.num_programs(0)
    n_groups = n_pages // group
    rows = t_new * heads
    chunk = LANES

    def copies(bb, g, slot):
        out = []
        base = n_pages - (g + 1) * group
        for p in range(group):
            phys = pt_ref[bb, base + p]
            dst = pl.ds(p * page, page)
            out.append(pltpu.make_async_copy(clat_hbm.at[layer, phys], lat_buf.at[slot, dst], sems.at[0, slot]))
            out.append(pltpu.make_async_copy(ckr_hbm.at[layer, phys], kr_buf.at[slot, :, dst], sems.at[1, slot]))
            out.append(pltpu.make_async_copy(ck_hbm.at[layer, phys], k_buf.at[slot, :, dst], sems.at[2, slot]))
            out.append(pltpu.make_async_copy(cv_hbm.at[layer, phys], v_buf.at[slot, :, dst], sems.at[3, slot]))
        return out

    @pl.when(b == 0)
    def _():
        for c in copies(0, 0, 0):
            c.start()

    def per_head_rows(x, width):
        x = x.astype(F32)
        full = jnp.concatenate(
            [jnp.broadcast_to(x[t:t + 1, :], (heads, x.shape[1])) for t in range(t_new)], axis=0)
        r = lax.broadcasted_iota(jnp.int32, full.shape, 0)
        c = lax.broadcasted_iota(jnp.int32, full.shape, 1)
        return jnp.where((c // width) == (r % heads), full, 0.0).astype(BF16)

    rope = kr_buf.shape[1]
    q_bd = per_head_rows(qpad_ref[...], LANES)
    q_abs = _dot_nt(q_bd, wuk_ref[...]).astype(BF16)
    q_rope = _dot(q_bd, sel_ref[...])[:, :rope].astype(BF16)
    sq_bd = per_head_rows(sqb_ref[...], sb_dim)
    tri = tri_ref[...]

    def process(lat, kr, kk, vv, carry, mask_mla=None, mask_sb=None):
        m, l, acc, o_sb, ls = carry
        nk = lat.shape[0]
        s = _dot_nt(q_abs, lat) + _dot(q_rope, kr)
        if mask_mla is not None:
            s = jnp.where(mask_mla, s, NEG_INF)
        m_new = jnp.maximum(m, jnp.max(s, axis=-1, keepdims=True))
        corr = jnp.exp(m - m_new)
        p = jnp.exp(s - m_new)
        l = l * corr + jnp.sum(p, axis=-1, keepdims=True)
        acc = acc * corr + _dot(p.astype(BF16), lat)

        z = _dot(sq_bd, kk)
        lneg = _neg_softplus(z)
        lpos = z + lneg
        if mask_sb is not None:
            lneg = jnp.where(mask_sb, lneg, 0.0)
        nch = nk // chunk
        hi, lo = _split_hi_lo(lneg)
        stacked = jnp.concatenate([hi[:, c * chunk:(c + 1) * chunk] for c in range(nch)]
                                  + [lo[:, c * chunk:(c + 1) * chunk] for c in range(nch)], axis=0)
        suf = _dot(stacked, tri)
        pieces = [None] * nch
        run = ls
        for c in range(nch - 1, -1, -1):
            within = suf[c * rows:(c + 1) * rows] + suf[(nch + c) * rows:(nch + c + 1) * rows]
            pieces[c] = within + run
            run = run + jnp.sum(lneg[:, c * chunk:(c + 1) * chunk], axis=-1, keepdims=True)
        suffix = pieces[0] if nch == 1 else jnp.concatenate(pieces, axis=1)
        w = jnp.exp(lpos + suffix)
        if mask_sb is not None:
            w = jnp.where(mask_sb, w, 0.0)
        o_sb = o_sb + _dot_nt(w.astype(BF16), vv)
        return m_new, l, acc, o_sb, run

    r_i = lax.broadcasted_iota(jnp.int32, (rows, chunk), 0) // heads
    c_i = lax.broadcasted_iota(jnp.int32, (rows, chunk), 1)
    valid = c_i < t_new
    init = (jnp.full((rows, 1), NEG_INF, F32), jnp.zeros((rows, 1), F32), jnp.zeros((rows, kv_rank), F32),
            jnp.zeros((rows, sb_heads * sb_dim), F32), jnp.zeros((rows, 1), F32))
    carry = process(nlat_ref[...].astype(BF16), nkr_ref[...].astype(BF16), nk_ref[...].astype(BF16),
                    nv_ref[...].astype(BF16), init,
                    mask_mla=valid & (r_i >= c_i), mask_sb=valid & (r_i > c_i))

    def group_step(g, carry):
        slot = g % 2
        nxt = 1 - slot

        @pl.when(g + 1 < n_groups)
        def _():
            for c in copies(b, g + 1, nxt):
                c.start()

        @pl.when((g + 1 == n_groups) & (b + 1 < nb))
        def _():
            for c in copies(b + 1, 0, nxt):
                c.start()

        for c in copies(b, g, slot):
            c.wait()
        return process(lat_buf[slot].astype(BF16), kr_buf[slot].astype(BF16),
                       k_buf[slot].astype(BF16), v_buf[slot].astype(BF16), carry)

    m, l, acc, o_sb, _ = lax.fori_loop(0, n_groups, group_step, carry)

    o_all = _dot((acc / l).astype(BF16), wuv_ref[...])

    def store_own_head(x, width, lane0):
        r = lax.broadcasted_iota(jnp.int32, x.shape, 0)
        c = lax.broadcasted_iota(jnp.int32, x.shape, 1)
        xm = jnp.where((c // width) == (r % heads), x, 0.0)
        for t in range(t_new):
            o_ref[t:t + 1, lane0:lane0 + heads * width] = jnp.sum(
                xm[t * heads:(t + 1) * heads], axis=0, keepdims=True)

    store_own_head(o_all, vdim, 0)
    store_own_head(o_sb, sb_dim, heads * vdim)


def _decode_attn(qpad, sqb, lat, kr, sk, sv, wuk_pad, sel, wuv2, caches, page_table, *,
                 dec_batch, t_new, dims, group, layer):
    heads, vdim = dims["heads"], dims["vdim"]
    sb_heads, sb_dim = dims["sb_heads"], dims["sb_dim"]
    kv_rank, rope = dims["kv_rank"], dims["rope"]
    sb_width = sb_heads * sb_dim
    c_lat, c_kr, c_k, c_v = caches
    page = c_lat.shape[2]
    n_pages = page_table.shape[1]
    assert heads == sb_heads and n_pages % group == 0 and (n_pages // group) % 2 == 0
    assert page == LANES and t_new <= LANES
    hw = heads * LANES
    mix_w = heads * vdim + sb_width

    def new_keys(x, transpose):
        x = jnp.pad(x.reshape(dec_batch, t_new, -1), ((0, 0), (0, page - t_new), (0, 0)))
        return jnp.transpose(x, (0, 2, 1)) if transpose else x

    nlat = new_keys(lat, False)
    nkr = new_keys(kr, True)
    nk = new_keys(sk, True)
    nv = new_keys(sv, True)
    per_b = lambda b, pt: (b, 0, 0)
    const2 = lambda b, pt: (0, 0)
    kern = functools.partial(_decode_kernel, layer=layer, t_new=t_new, heads=heads, vdim=vdim,
                             sb_heads=sb_heads, sb_dim=sb_dim, page=page, group=group, n_pages=n_pages,
                             kv_rank=kv_rank)
    gk = group * page
    grid_spec = pltpu.PrefetchScalarGridSpec(
        num_scalar_prefetch=1,
        grid=(dec_batch,),
        in_specs=[
            pl.BlockSpec((None, t_new, hw), per_b),
            pl.BlockSpec((None, t_new, sb_width), per_b),
            pl.BlockSpec((None, page, kv_rank), per_b),
            pl.BlockSpec((None, rope, page), per_b),
            pl.BlockSpec((None, sb_width, page), per_b),
            pl.BlockSpec((None, sb_width, page), per_b),
            pl.BlockSpec(wuk_pad.shape, const2),
            pl.BlockSpec(sel.shape, const2),
            pl.BlockSpec(wuv2.shape, const2),
            pl.BlockSpec((LANES, LANES), const2),
            pl.BlockSpec(memory_space=pl.ANY),
            pl.BlockSpec(memory_space=pl.ANY),
            pl.BlockSpec(memory_space=pl.ANY),
            pl.BlockSpec(memory_space=pl.ANY),
        ],
        out_specs=pl.BlockSpec((None, t_new, mix_w), per_b),
        scratch_shapes=[
            pltpu.VMEM((2, gk, kv_rank), F32),
            pltpu.VMEM((2, rope, gk), F32),
            pltpu.VMEM((2, sb_width, gk), F32),
            pltpu.VMEM((2, sb_width, gk), F32),
            pltpu.SemaphoreType.DMA((4, 2)),
        ],
    )
    out = pl.pallas_call(
        kern,
        grid_spec=grid_spec,
        out_shape=jax.ShapeDtypeStruct((dec_batch, t_new, mix_w), F32),
        compiler_params=_cparams(("arbitrary",)),
        name="decode_attn",
    )(page_table, qpad.reshape(dec_batch, t_new, hw), sqb.reshape(dec_batch, t_new, sb_width),
      nlat, nkr, nk, nv, wuk_pad, sel, wuv2, _strict_lower(LANES), c_lat, c_kr, c_k, c_v)
    return out.reshape(dec_batch * t_new, mix_w)


def _finish_a_kernel(x_ref, mix_ref, wout_ref, g_ref, b_ref, wpq_ref, keys_ref,
                     h1_ref, h1t_ref, rank_ref, e1_ref, cnt_ref, sc_ref, v0_ref, v1_ref, *,
                     alpha, p_heads, n_keys, topk):
    h1 = _layernorm(alpha * x_ref[...] + _dot(mix_ref[...].astype(BF16), wout_ref[...]),
                    g_ref[...], b_ref[...])
    h1_ref[...] = h1
    h1t_ref[...] = h1.T.astype(BF16)
    qp = _dot(h1.astype(BF16), wpq_ref[...]).astype(BF16)
    qhalf = qp.shape[1] // (2 * p_heads)

    def top_vals(s, dst_ref, want_rank):
        cur = s
        rank = jnp.full(s.shape, float(topk), F32) if want_rank else None
        for r in range(topk):
            mx = jnp.max(cur, axis=0, keepdims=True)
            dst_ref[r:r + 1, :] = mx
            hit = cur == mx
            if want_rank:
                rank = jnp.where(hit, float(r), rank)
            cur = jnp.where(hit, NEG_INF, cur)
        return rank

    for h in range(p_heads):
        s0 = _dot_nt(keys_ref[2 * h], qp[:, (2 * h) * qhalf:(2 * h + 1) * qhalf])
        s1 = _dot_nt(keys_ref[2 * h + 1], qp[:, (2 * h + 1) * qhalf:(2 * h + 2) * qhalf])
        top_vals(s0, v0_ref, False)
        rank_ref[h] = _pack_rows(top_vals(s1, v1_ref, True))
        v0 = v0_ref[...]
        v1 = v1_ref[...]
        half = topk // 2
        parts = [v0[0:1, :] + v1]
        for a in range(1, half):
            parts.append(v0[a:a + 1, :] + v1[0:half, :])
        parts.append(v0[half:topk, :] + v1[0:1, :])
        cand = jnp.concatenate(parts, axis=0)
        cur = cand
        tau = None
        for r in range(topk):
            tau = jnp.max(cur, axis=0, keepdims=True)
            cur = jnp.where(cur == tau, NEG_INF, cur)
        m0 = v0[0:1, :]
        m1 = v1[0:1, :]
        zsum = jnp.sum(jnp.where(cand >= tau, jnp.exp(cand - (m0 + m1)), 0.0), axis=0, keepdims=True)
        thr = tau - s0
        cnt = jnp.zeros_like(s0)
        for b in range(topk):
            cnt = cnt + jnp.where(v1[b:b + 1, :] >= thr, 1.0, 0.0)
        cnt_ref[h] = cnt
        sc_ref[h] = jnp.exp(s0 - m0) * (1.0 / zsum)
        e1_ref[h] = _pack_rows(jnp.exp(s1 - m1))


def _finish_a(x, mix, wout, g, b, wpq, keys, *, alpha, tm):
    n, d = x.shape
    n2, n_keys, _ = keys.shape
    p_heads = n2 // 2
    row = lambda i: (i, 0)
    kern = functools.partial(_finish_a_kernel, alpha=alpha, p_heads=p_heads, n_keys=n_keys, topk=PEER_TOPK)
    return pl.pallas_call(
        kern,
        grid=(n // tm,),
        in_specs=[pl.BlockSpec((tm, d), row), pl.BlockSpec((tm, mix.shape[1]), row), _full(wout.shape),
                  _full(g.shape), _full(b.shape), _full(wpq.shape), _full(keys.shape)],
        out_specs=(pl.BlockSpec((tm, d), row),
                   pl.BlockSpec((d, tm), lambda i: (0, i)))
        + tuple(pl.BlockSpec((p_heads, rows, tm), lambda i: (0, 0, i))
                for rows in (n_keys // 2, n_keys // 2, n_keys, n_keys)),
        out_shape=(jax.ShapeDtypeStruct((n, d), F32),
                   jax.ShapeDtypeStruct((d, n), BF16),
                   jax.ShapeDtypeStruct((p_heads, n_keys // 2, n), jnp.uint32),
                   jax.ShapeDtypeStruct((p_heads, n_keys // 2, n), jnp.uint32),
                   jax.ShapeDtypeStruct((p_heads, n_keys, n), F32),
                   jax.ShapeDtypeStruct((p_heads, n_keys, n), F32)),
        scratch_shapes=[pltpu.VMEM((PEER_TOPK, tm), F32), pltpu.VMEM((PEER_TOPK, tm), F32)],
        compiler_params=_cparams(("parallel",)),
        name="finish_a",
    )(x, mix, wout, g, b, wpq, keys)


def _peer_dense_kernel(h1t_ref, rank_ref, e1_ref, cnt_ref, sc_ref, u_ref, vt_ref, o_ref, p_ref, *,
                       p_heads, n_keys, et, tt, sub):
    e = pl.program_id(1)

    @pl.when(e == 0)
    def _():
        o_ref[...] = jnp.zeros_like(o_ref)

    def key_row(rows, ii):
        packed = jnp.broadcast_to(rows[ii:ii + 1, :], (16, LANES)).astype(BF16)
        return jnp.concatenate([packed] * (n_keys // 16), axis=0)

    per_tile = et // n_keys
    inv_sqrt2 = 1.0 / math.sqrt(2.0)
    i0 = pl.multiple_of(e * per_tile, per_tile)
    h1t = h1t_ref[...]
    out = None

    def pre_act(q):
        return _dot(u_ref[q * sub:(q + 1) * sub, :], h1t)

    n_sub = et // sub
    st_next = pre_act(0)
    for q in range(n_sub):
        qs = slice(q * sub, (q + 1) * sub)
        st = st_next
        if q + 1 < n_sub:
            st_next = pre_act(q + 1)
        for c in range(tt // LANES):
            cs = slice(c * LANES, (c + 1) * LANES)
            cnt_rows = [cnt_ref[h, pl.ds(i0, per_tile), cs] for h in range(p_heads)]
            sc_rows = [sc_ref[h, pl.ds(i0, per_tile), cs] for h in range(p_heads)]
            for r in range(sub // n_keys):
                ii = q * (sub // n_keys) + r
                w = jnp.zeros((n_keys, LANES), BF16)
                for h in range(p_heads):
                    cnt = key_row(cnt_rows[h], ii)
                    sc = key_row(sc_rows[h], ii)
                    rank = pltpu.bitcast(rank_ref[h, :, cs], BF16)
                    e1 = pltpu.bitcast(e1_ref[h, :, cs], BF16)
                    w = w + jnp.where(rank < cnt, e1 * sc, jnp.zeros_like(sc))
                s = st[r * n_keys:(r + 1) * n_keys, cs]
                act = 0.5 * s * (1.0 + lax.erf(s * inv_sqrt2))
                p_ref[ii * n_keys:(ii + 1) * n_keys, cs] = act.astype(BF16) * w
        part = _dot(vt_ref[:, qs], p_ref[qs, :])
        out = part if out is None else out + part
    o_ref[...] += out


def _peer_dense(h1t, rank, e1, cnt, sc, u_b, vt_b, *, tt, et):
    d, n = h1t.shape
    p_heads, n_keys, _ = cnt.shape
    ne = u_b.shape[0]
    assert tt % LANES == 0 and et % (8 * n_keys) == 0 and ne % et == 0 and n_keys % 16 == 0
    kern = functools.partial(_peer_dense_kernel, p_heads=p_heads, n_keys=n_keys, et=et, tt=tt,
                             sub=2 * n_keys)
    route = pl.BlockSpec((p_heads, n_keys, tt), lambda i, e: (0, 0, i))
    packed = pl.BlockSpec((p_heads, n_keys // 2, tt), lambda i, e: (0, 0, i))
    return pl.pallas_call(
        kern,
        grid=(n // tt, ne // et),
        in_specs=[pl.BlockSpec((d, tt), lambda i, e: (0, i)), packed, packed, route, route,
                  pl.BlockSpec((et, d), lambda i, e: (e, 0)),
                  pl.BlockSpec((d, et), lambda i, e: (0, e))],
        out_specs=pl.BlockSpec((d, tt), lambda i, e: (0, i)),
        out_shape=jax.ShapeDtypeStruct((d, n), F32),
        scratch_shapes=[pltpu.VMEM((et, tt), BF16)],
        compiler_params=_cparams(("parallel", "arbitrary")),
        name="peer_dense",
    )(h1t, rank, e1, cnt, sc, u_b, vt_b)


def _finish_b_kernel(h1_ref, pt_ref, p_ref, g_ref, b_ref, wg_ref, bg_ref, wp_ref, y_ref, *, alpha):
    h2 = _layernorm(alpha * h1_ref[...] + pt_ref[...].T, g_ref[...], b_ref[...])
    gate = jax.nn.sigmoid(_dot(h2.astype(BF16), wg_ref[...]) + bg_ref[...])
    y_ref[...] = h2 + _dot(p_ref[...].astype(BF16), wp_ref[...]) * gate


def _finish_b(h1, peer_t, p, g, b, wg, bg, wp, *, alpha, tm):
    n, d = h1.shape
    row = lambda i: (i, 0)
    return pl.pallas_call(
        functools.partial(_finish_b_kernel, alpha=alpha),
        grid=(n // tm,),
        in_specs=[pl.BlockSpec((tm, d), row), pl.BlockSpec((d, tm), lambda i: (0, i)),
                  pl.BlockSpec((tm, p.shape[1]), row), _full(g.shape), _full(b.shape), _full(wg.shape),
                  _full(bg.shape), _full(wp.shape)],
        out_specs=pl.BlockSpec((tm, d), row),
        out_shape=jax.ShapeDtypeStruct((n, d), F32),
        compiler_params=_cparams(("parallel",)),
        name="finish_b",
    )(h1, peer_t, p, g, b, wg, bg, wp)


def _rope_table(pos, rope, nope, scale):
    half = rope // 2
    inv = ROPE_THETA ** (-jnp.arange(half, dtype=F32) / half)
    ang = pos.astype(F32)[:, None] * inv[None, :]
    cos, sin = jnp.cos(ang), jnp.sin(ang)
    n = pos.shape[0]
    zeros = lambda w: jnp.zeros((n, w), F32)
    cq = jnp.concatenate([jnp.ones((n, nope), F32), cos, cos, zeros(LANES - nope - rope)], axis=1) * scale
    sq = jnp.concatenate([zeros(nope), -sin, sin, zeros(LANES - nope - rope)], axis=1) * scale
    ck = jnp.concatenate([cos, cos, zeros(LANES - rope)], axis=1)
    sk = jnp.concatenate([-sin, sin, zeros(LANES - rope)], axis=1)
    return jnp.concatenate([cq, sq, ck, sk], axis=1)


def _swap_halves(w):
    half = w.shape[-1] // 2
    return jnp.concatenate([w[..., half:], w[..., :half]], axis=-1)


def _prep_layer(w_in, w_uq, w_uk, w_uv, dims):
    heads, nope, rope, vdim = dims["heads"], dims["nope"], dims["rope"], dims["vdim"]
    q_rank, kv_rank, sb_width = dims["q_rank"], dims["kv_rank"], dims["sb_width"]
    d = w_in.shape[0]
    cuts = np.cumsum([q_rank, kv_rank, rope, sb_width, sb_width])
    cq, ckv, kr, sq, sk, sv = jnp.split(w_in, [int(c) for c in cuts], axis=1)
    padl = lambda w: jnp.pad(w, ((0, 0), (0, LANES - w.shape[1])))
    w1 = jnp.concatenate([cq, ckv, sq, sk, sv, padl(kr), padl(_swap_halves(kr))], axis=1).astype(BF16)

    hd = nope + rope
    wq3 = w_uq.reshape(q_rank, heads, hd)
    pad3 = lambda w: jnp.pad(w, ((0, 0), (0, 0), (0, LANES - w.shape[2])))
    main = pad3(wq3)
    swap = pad3(jnp.concatenate([jnp.zeros((q_rank, heads, nope), F32), _swap_halves(wq3[..., nope:])], axis=2))
    wq = jnp.concatenate([main.reshape(q_rank, -1), swap.reshape(q_rank, -1)], axis=1).astype(BF16)

    wuk_pad = pad3(w_uk).reshape(kv_rank, heads * LANES)
    wkv = jnp.concatenate([wuk_pad, w_uv.reshape(kv_rank, heads * vdim)], axis=1).astype(BF16)

    e = np.zeros((LANES, heads * LANES), np.float32)
    sel = np.zeros((heads * LANES, LANES), np.float32)
    for h in range(heads):
        for r in range(rope):
            e[r, h * LANES + nope + r] = 1.0
            sel[h * LANES + nope + r, r] = 1.0
    return dict(w1=w1, wq=wq, wkv=wkv, emat=jnp.asarray(e, BF16), sel=jnp.asarray(sel, BF16),
                wuk_pad=wuk_pad.astype(BF16), wuv2=w_uv.reshape(kv_rank, heads * vdim).astype(BF16))


def _pick_tile(n, pref):
    t = pref
    while n % t:
        t //= 2
    return t


def kernel(x_prompt, x_sample, cache_mla_latent, cache_mla_krope, cache_sb_k, cache_sb_v, page_table,
           p_prompt, p_sample, w_in, g_q_norm, g_kv_norm, w_uq, w_uk, w_uv, w_out, ln1_g, ln1_b,
           peer_w_q, peer_sub_keys, peer_u, peer_v, ln2_g, ln2_b, ple_w_proj, ple_w_gate, ple_b_gate):
    depth = w_in.shape[0]
    batch, seq, d = x_prompt.shape
    dec_batch, t_new, _ = x_sample.shape
    n_pool, page = cache_mla_latent.shape[1:3]
    kv_rank, heads, nope = w_uk.shape[1:]
    vdim = w_uv.shape[3]
    rope = cache_mla_krope.shape[3]
    sb_heads, sb_dim = cache_sb_k.shape[3:]
    dims = dict(heads=heads, nope=nope, rope=rope, vdim=vdim, q_rank=w_uq.shape[1], kv_rank=kv_rank,
                sb_heads=sb_heads, sb_dim=sb_dim, sb_width=sb_heads * sb_dim,
                sb_scale=1.0 / math.sqrt(sb_dim))
    assert nope + rope <= LANES and rope % 2 == 0
    mla_scale = 1.0 / math.sqrt(nope + rope)
    alpha = (2 * depth) ** 0.25
    past_len = page_table.shape[1] * page
    n_p, n_s = batch * seq, dec_batch * t_new

    tab_p = _rope_table(jnp.arange(seq), rope, nope, mla_scale)
    tab_s = jnp.tile(_rope_table(past_len + jnp.arange(t_new), rope, nope, mla_scale), (dec_batch, 1))

    keys_last = lambda c: jnp.moveaxis(c, 2, -1).reshape(depth, n_pool, -1, page)
    caches = (cache_mla_latent, keys_last(cache_mla_krope), keys_last(cache_sb_k), keys_last(cache_sb_v))

    hp = x_prompt.reshape(n_p, d)
    hs = x_sample.reshape(n_s, d)
    outs_p, outs_s = [], []
    for i in range(depth):
        lw = _prep_layer(w_in[i], w_uq[i], w_uk[i], w_uv[i], dims)
        gq, gkv = g_q_norm[i][None, :], g_kv_norm[i][None, :]
        wout_b = w_out[i].astype(BF16)
        wpq_b = peer_w_q[i].astype(BF16)
        keys_b = peer_sub_keys[i].reshape(-1, *peer_sub_keys.shape[3:]).astype(BF16)
        u_b = peer_u[i].astype(BF16)
        vt_b = peer_v[i].T.astype(BF16)
        wg_b = ple_w_gate[i].astype(BF16)
        wp_b = ple_w_proj[i].astype(BF16)
        row = lambda v: v[None, :]

        def finish(h, mix, p):
            n = h.shape[0]
            tm = _pick_tile(n, 256)
            h1, h1t, rank, e1, cnt, sc = _finish_a(h, mix, wout_b, row(ln1_g[i]), row(ln1_b[i]), wpq_b,
                                                   keys_b, alpha=alpha, tm=tm)
            peer_t = _peer_dense(h1t, rank, e1, cnt, sc, u_b, vt_b, tt=_pick_tile(n, 512),
                                 et=_pick_tile(u_b.shape[0], 2048))
            return _finish_b(h1, peer_t, p.reshape(n, -1), row(ln2_g[i]), row(ln2_b[i]), wg_b,
                             row(ple_b_gate[i]), wp_b, alpha=alpha, tm=tm)

        lat, kr, sk, sv, qpad, _, _, sqb, _, _ = _inproj(
            hs, tab_s, lw["w1"], gq, gkv, lw["wq"], lw["wkv"], lw["emat"], dims=dims, tm=_pick_tile(n_s, 256))
        mix = _decode_attn(qpad, sqb, lat, kr, sk, sv, lw["wuk_pad"], lw["sel"], lw["wuv2"], caches,
                           page_table, dec_batch=dec_batch, t_new=t_new, dims=dims,
                           group=math.gcd(8, page_table.shape[1] // 2), layer=i)
        outs_s.append((lat, kr, sk, sv))
        hs = finish(hs, mix, p_sample[i])

        lat, kr, sk, sv, qpad, kpad, vb, sqb, skb, svb = _inproj(
            hp, tab_p, lw["w1"], gq, gkv, lw["wq"], lw["wkv"], lw["emat"], dims=dims, tm=_pick_tile(seq, 256))
        mix = _prompt_attn(qpad, kpad, vb, sqb, skb, svb, batch=batch, seq=seq, dims=dims,
                           blk=_pick_tile(seq, 256))
        outs_p.append((lat, kr, sk, sv))
        hp = finish(hp, mix, p_prompt[i])

    def stack(outs, k, shape):
        return jnp.stack([o[k].reshape(shape) for o in outs])

    return (hp.reshape(batch, seq, d), hs.reshape(dec_batch, t_new, d),
            stack(outs_p, 0, (batch, seq, kv_rank)), stack(outs_p, 1, (batch, seq, rope)),
            stack(outs_p, 2, (batch, seq, sb_heads, sb_dim)), stack(outs_p, 3, (batch, seq, sb_heads, sb_dim)),
            stack(outs_s, 0, (dec_batch, t_new, kv_rank)), stack(outs_s, 1, (dec_batch, t_new, rope)),
            stack(outs_s, 2, (dec_batch, t_new, sb_heads, sb_dim)),
            stack(outs_s, 3, (dec_batch, t_new, sb_heads, sb_dim)))
```

```python
import functools
import math

import jax
import jax.numpy as jnp
import numpy as np
from jax import lax
from jax.experimental import pallas as pl
from jax.experimental.pallas import tpu as pltpu

F32 = jnp.float32
BF16 = jnp.bfloat16

ROPE_THETA = 10000.0
EPS = 1e-6
PEER_TOPK = 16
LANES = 128
VMEM_LIMIT = 56 * 1024 * 1024
NEG_INF = float("-inf")


def _cparams(sem):
    return pltpu.CompilerParams(dimension_semantics=sem, vmem_limit_bytes=VMEM_LIMIT)


def _full(shape):
    n = len(shape)
    return pl.BlockSpec(shape, lambda *_: (0,) * n)


def _dot(a, b):
    return jnp.dot(a, b, preferred_element_type=F32)


def _dot_nt(a, b):
    return lax.dot_general(a, b, (((1,), (1,)), ((), ())), preferred_element_type=F32)


def _layernorm(x, g, b):
    mu = jnp.mean(x, axis=-1, keepdims=True)
    xc = x - mu
    var = jnp.mean(xc * xc, axis=-1, keepdims=True)
    return xc * lax.rsqrt(var + EPS) * g + b


def _rmsnorm(x, g):
    return x * lax.rsqrt(jnp.mean(x * x, axis=-1, keepdims=True) + EPS) * g


def _neg_softplus(z):
    return -(jnp.maximum(z, 0.0) + jnp.log(1.0 + jnp.exp(-jnp.abs(z))))


def _pack_rows(x):
    return pltpu.bitcast(x.astype(BF16), jnp.uint32)


def _split_hi_lo(x):
    hi = x.astype(BF16)
    lo = (x - hi.astype(F32)).astype(BF16)
    return hi, lo


def _inproj_kernel(x_ref, tab_ref, w1_ref, gq_ref, gkv_ref, wq_ref, wkv_ref, e_ref,
                   lat_ref, kr_ref, sk_ref, sv_ref, qpad_ref, kpad_ref, vb_ref, sqb_ref, skb_ref,
                   svb_ref, *, q_rank, kv_rank, sb_width, rope, heads, sb_scale):
    xb = x_ref[...].astype(BF16)
    proj = _dot(xb, w1_ref[...])
    c1 = q_rank
    c2 = c1 + kv_rank
    c3 = c2 + sb_width
    c4 = c3 + sb_width
    c5 = c4 + sb_width
    cq = proj[:, :c1]
    ckv = proj[:, c1:c2]
    sq = proj[:, c2:c3]
    sk = proj[:, c3:c4]
    sv = proj[:, c4:c5]
    kr_a = proj[:, c5:c5 + LANES]
    kr_b = proj[:, c5 + LANES:c5 + 2 * LANES]

    tab = tab_ref[...]
    cq_t = tab[:, 0:LANES]
    sq_t = tab[:, LANES:2 * LANES]
    ck_t = tab[:, 2 * LANES:3 * LANES]
    sk_t = tab[:, 3 * LANES:4 * LANES]

    cqn = _rmsnorm(cq, gq_ref[...]).astype(BF16)
    qq = _dot(cqn, wq_ref[...])
    hw = heads * LANES
    for h in range(heads):
        sl = slice(h * LANES, (h + 1) * LANES)
        sl2 = slice(hw + h * LANES, hw + (h + 1) * LANES)
        qpad_ref[:, sl] = (qq[:, sl] * cq_t + qq[:, sl2] * sq_t).astype(BF16)

    lat = _rmsnorm(ckv, gkv_ref[...])
    lat_ref[...] = lat
    kv = _dot(lat.astype(BF16), wkv_ref[...])
    krot = kr_a * ck_t + kr_b * sk_t
    kr_ref[...] = krot[:, :rope]
    kpad_ref[...] = (kv[:, :hw] + _dot(krot.astype(BF16), e_ref[...])).astype(BF16)
    vb_ref[...] = kv[:, hw:].astype(BF16)
    sqb_ref[...] = (sq * sb_scale).astype(BF16)
    skb_ref[...] = sk.astype(BF16)
    svb_ref[...] = sv.astype(BF16)
    sk_ref[...] = sk
    sv_ref[...] = sv


def _inproj(x, tab, w1, gq, gkv, wq, wkv, emat, *, dims, tm):
    n, d = x.shape
    heads, nope, rope, vdim = dims["heads"], dims["nope"], dims["rope"], dims["vdim"]
    q_rank, kv_rank, sb_width = dims["q_rank"], dims["kv_rank"], dims["sb_width"]
    hw = heads * LANES
    period = tab.shape[0] // tm
    row = lambda i: (i, 0)
    out_shapes = (
        jax.ShapeDtypeStruct((n, kv_rank), F32),
        jax.ShapeDtypeStruct((n, rope), F32),
        jax.ShapeDtypeStruct((n, sb_width), F32),
        jax.ShapeDtypeStruct((n, sb_width), F32),
        jax.ShapeDtypeStruct((n, hw), BF16),
        jax.ShapeDtypeStruct((n, hw), BF16),
        jax.ShapeDtypeStruct((n, heads * vdim), BF16),
        jax.ShapeDtypeStruct((n, sb_width), BF16),
        jax.ShapeDtypeStruct((n, sb_width), BF16),
        jax.ShapeDtypeStruct((n, sb_width), BF16),
    )
    kern = functools.partial(_inproj_kernel, q_rank=q_rank, kv_rank=kv_rank, sb_width=sb_width,
                             rope=rope, heads=heads, sb_scale=dims["sb_scale"])
    return pl.pallas_call(
        kern,
        grid=(n // tm,),
        in_specs=[
            pl.BlockSpec((tm, d), row),
            pl.BlockSpec((tm, 4 * LANES), lambda i: (i % period, 0)),
            _full(w1.shape), _full(gq.shape), _full(gkv.shape), _full(wq.shape), _full(wkv.shape),
            _full(emat.shape),
        ],
        out_specs=tuple(pl.BlockSpec((tm, s.shape[1]), row) for s in out_shapes),
        out_shape=out_shapes,
        compiler_params=_cparams(("parallel",)),
        name="inproj",
    )(x, tab, w1, gq, gkv, wq, wkv, emat)


def _prompt_attn_kernel(q_ref, k_ref, v_ref, sq_ref, sk_ref, sv_ref, tri_ref, o_ref, *,
                        blk, heads, vdim, sb_heads, sb_dim, hgroup):
    i = pl.program_id(1)
    row = lax.broadcasted_iota(jnp.int32, (blk, blk), 0)
    col = lax.broadcasted_iota(jnp.int32, (blk, blk), 1)
    tri = tri_ref[...]
    mla_w = heads * vdim

    def mla_step(h, off, carry, masked):
        m, l, acc = carry
        qh = q_ref[:, h * LANES:(h + 1) * LANES]
        kh = k_ref[pl.ds(off, blk), h * LANES:(h + 1) * LANES]
        vh = v_ref[pl.ds(off, blk), h * vdim:(h + 1) * vdim]
        s = _dot_nt(qh, kh)
        if masked:
            s = jnp.where(row >= col, s, NEG_INF)
        m_new = jnp.maximum(m, jnp.max(s, axis=-1, keepdims=True))
        corr = jnp.exp(m - m_new)
        p = jnp.exp(s - m_new)
        l = l * corr + jnp.sum(p, axis=-1, keepdims=True)
        acc = acc * corr + _dot(p.astype(BF16), vh)
        return m_new, l, acc

    def sb_step(h, off, carry, masked):
        log_carry, acc = carry
        sqh = sq_ref[:, h * sb_dim:(h + 1) * sb_dim]
        kh = sk_ref[pl.ds(off, blk), h * sb_dim:(h + 1) * sb_dim]
        vh = sv_ref[pl.ds(off, blk), h * sb_dim:(h + 1) * sb_dim]
        z = _dot_nt(sqh, kh)
        lneg = _neg_softplus(z)
        lpos = z + lneg
        if masked:
            lneg = jnp.where(row > col, lneg, 0.0)
        hi, lo = _split_hi_lo(lneg)
        suffix = _dot(jnp.concatenate([hi, lo], axis=1), tri)
        w = jnp.exp(lpos + suffix + log_carry)
        if masked:
            w = jnp.where(row > col, w, 0.0)
        acc = acc + _dot(w.astype(BF16), vh)
        log_carry = log_carry + jnp.sum(lneg, axis=-1, keepdims=True)
        return log_carry, acc

    mla_init = (jnp.full((blk, 1), NEG_INF, F32), jnp.zeros((blk, 1), F32), jnp.zeros((blk, vdim), F32))
    sb_init = (jnp.zeros((blk, 1), F32), jnp.zeros((blk, sb_dim), F32))
    for h0 in range(0, heads, hgroup):
        hs = tuple(range(h0, h0 + hgroup))

        def step(j, carry, masked, hs=hs):
            off = pl.multiple_of((i - j) * blk, blk)
            mla_c, sb_c = carry
            return (tuple(mla_step(h, off, c, masked) for h, c in zip(hs, mla_c)),
                    tuple(sb_step(h, off, c, masked) for h, c in zip(hs, sb_c)))

        carry = step(0, ((mla_init,) * hgroup, (sb_init,) * hgroup), True)
        mla_c, sb_c = lax.fori_loop(1, i + 1, functools.partial(step, masked=False), carry)
        for h, (_, l, acc) in zip(hs, mla_c):
            o_ref[:, h * vdim:(h + 1) * vdim] = (acc / l).astype(o_ref.dtype)
        for h, (_, acc) in zip(hs, sb_c):
            o_ref[:, mla_w + h * sb_dim:mla_w + (h + 1) * sb_dim] = acc.astype(o_ref.dtype)


def _strict_lower(n):
    r = np.arange(n)
    return jnp.asarray((r[:, None] > r[None, :]).astype(np.float32), dtype=BF16)


def _prompt_attn(qpad, kpad, vb, sqb, skb, svb, *, batch, seq, dims, blk):
    heads, vdim = dims["heads"], dims["vdim"]
    sb_heads, sb_dim = dims["sb_heads"], dims["sb_dim"]
    nq = seq // blk
    hw = heads * LANES
    mix_w = heads * vdim + sb_heads * sb_dim
    qrow = lambda b, i: (b * nq + i, 0)
    per_b = lambda b, i: (b, 0, 0)
    assert heads == sb_heads
    kern = functools.partial(_prompt_attn_kernel, blk=blk, heads=heads, vdim=vdim,
                             sb_heads=sb_heads, sb_dim=sb_dim, hgroup=math.gcd(heads, 8))
    return pl.pallas_call(
        kern,
        grid=(batch, nq),
        in_specs=[
            pl.BlockSpec((blk, hw), qrow),
            pl.BlockSpec((None, seq, hw), per_b),
            pl.BlockSpec((None, seq, heads * vdim), per_b),
            pl.BlockSpec((blk, sb_heads * sb_dim), qrow),
            pl.BlockSpec((None, seq, sb_heads * sb_dim), per_b),
            pl.BlockSpec((None, seq, sb_heads * sb_dim), per_b),
            _full((2 * blk, blk)),
        ],
        out_specs=pl.BlockSpec((blk, mix_w), qrow),
        out_shape=jax.ShapeDtypeStruct((batch * seq, mix_w), BF16),
        compiler_params=_cparams(("parallel", "arbitrary")),
        name="prompt_attn",
    )(qpad, kpad.reshape(batch, seq, hw), vb.reshape(batch, seq, -1), sqb,
      skb.reshape(batch, seq, -1), svb.reshape(batch, seq, -1),
      jnp.concatenate([_strict_lower(blk)] * 2, axis=0))


def _decode_kernel(pt_ref, qpad_ref, sqb_ref, nlat_ref, nkr_ref, nk_ref, nv_ref,
                   wuk_ref, sel_ref, wuv_ref, tri_ref,
                   clat_hbm, ckr_hbm, ck_hbm, cv_hbm,
                   o_ref,
                   lat_buf, kr_buf, k_buf, v_buf, sems, *,
                   layer, t_new, heads, vdim, sb_heads, sb_dim, page, group, n_pages, kv_rank):
    b = pl.program_id(0)
    nb = pl.num_programs(0)
    n_groups = n_pages // group
    rows = t_new * heads
    chunk = LANES

    def copies(bb, g, slot):
        out = []
        base = n_pages - (g + 1) * group
        for p in range(group):
            phys = pt_ref[bb, base + p]
            dst = pl.ds(p * page, page)
            out.append(pltpu.make_async_copy(clat_hbm.at[layer, phys], lat_buf.at[slot, dst], sems.at[0, slot]))
            out.append(pltpu.make_async_copy(ckr_hbm.at[layer, phys], kr_buf.at[slot, :, dst], sems.at[1, slot]))
            out.append(pltpu.make_async_copy(ck_hbm.at[layer, phys], k_buf.at[slot, :, dst], sems.at[2, slot]))
            out.append(pltpu.make_async_copy(cv_hbm.at[layer, phys], v_buf.at[slot, :, dst], sems.at[3, slot]))
        return out

    @pl.when(b == 0)
    def _():
        for c in copies(0, 0, 0):
            c.start()

    def per_head_rows(x, width):
        x = x.astype(F32)
        full = jnp.concatenate(
            [jnp.broadcast_to(x[t:t + 1, :], (heads, x.shape[1])) for t in range(t_new)], axis=0)
        r = lax.broadcasted_iota(jnp.int32, full.shape, 0)
        c = lax.broadcasted_iota(jnp.int32, full.shape, 1)
        return jnp.where((c // width) == (r % heads), full, 0.0).astype(BF16)

    rope = kr_buf.shape[1]
    q_bd = per_head_rows(qpad_ref[...], LANES)
    q_abs = _dot_nt(q_bd, wuk_ref[...]).astype(BF16)
    q_rope = _dot(q_bd, sel_ref[...])[:, :rope].astype(BF16)
    sq_bd = per_head_rows(sqb_ref[...], sb_dim)
    tri = tri_ref[...]

    def process(lat, kr, kk, vv, carry, mask_mla=None, mask_sb=None):
        m, l, acc, o_sb, ls = carry
        nk = lat.shape[0]
        s = _dot_nt(q_abs, lat) + _dot(q_rope, kr)
        if mask_mla is not None:
            s = jnp.where(mask_mla, s, NEG_INF)
        m_new = jnp.maximum(m, jnp.max(s, axis=-1, keepdims=True))
        corr = jnp.exp(m - m_new)
        p = jnp.exp(s - m_new)
        l = l * corr + jnp.sum(p, axis=-1, keepdims=True)
        acc = acc * corr + _dot(p.astype(BF16), lat)

        z = _dot(sq_bd, kk)
        lneg = _neg_softplus(z)
        lpos = z + lneg
        if mask_sb is not None:
            lneg = jnp.where(mask_sb, lneg, 0.0)
        nch = nk // chunk
        hi, lo = _split_hi_lo(lneg)
        stacked = jnp.concatenate([hi[:, c * chunk:(c + 1) * chunk] for c in range(nch)]
                                  + [lo[:, c * chunk:(c + 1) * chunk] for c in range(nch)], axis=0)
        suf = _dot(stacked, tri)
        pieces = [None] * nch
        run = ls
        for c in range(nch - 1, -1, -1):
            within = suf[c * rows:(c + 1) * rows] + suf[(nch + c) * rows:(nch + c + 1) * rows]
            pieces[c] = within + run
            run = run + jnp.sum(lneg[:, c * chunk:(c + 1) * chunk], axis=-1, keepdims=True)
        suffix = pieces[0] if nch == 1 else jnp.concatenate(pieces, axis=1)
        w = jnp.exp(lpos + suffix)
        if mask_sb is not None:
            w = jnp.where(mask_sb, w, 0.0)
        o_sb = o_sb + _dot_nt(w.astype(BF16), vv)
        return m_new, l, acc, o_sb, run

    r_i = lax.broadcasted_iota(jnp.int32, (rows, chunk), 0) // heads
    c_i = lax.broadcasted_iota(jnp.int32, (rows, chunk), 1)
    valid = c_i < t_new
    init = (jnp.full((rows, 1), NEG_INF, F32), jnp.zeros((rows, 1), F32), jnp.zeros((rows, kv_rank), F32),
            jnp.zeros((rows, sb_heads * sb_dim), F32), jnp.zeros((rows, 1), F32))
    carry = process(nlat_ref[...].astype(BF16), nkr_ref[...].astype(BF16), nk_ref[...].astype(BF16),
                    nv_ref[...].astype(BF16), init,
                    mask_mla=valid & (r_i >= c_i), mask_sb=valid & (r_i > c_i))

    def group_step(g, carry):
        slot = g % 2
        nxt = 1 - slot

        @pl.when(g + 1 < n_groups)
        def _():
            for c in copies(b, g + 1, nxt):
                c.start()

        @pl.when((g + 1 == n_groups) & (b + 1 < nb))
        def _():
            for c in copies(b + 1, 0, nxt):
                c.start()

        for c in copies(b, g, slot):
            c.wait()
        return process(lat_buf[slot].astype(BF16), kr_buf[slot].astype(BF16),
                       k_buf[slot].astype(BF16), v_buf[slot].astype(BF16), carry)

    m, l, acc, o_sb, _ = lax.fori_loop(0, n_groups, group_step, carry)

    o_all = _dot((acc / l).astype(BF16), wuv_ref[...])

    def store_own_head(x, width, lane0):
        r = lax.broadcasted_iota(jnp.int32, x.shape, 0)
        c = lax.broadcasted_iota(jnp.int32, x.shape, 1)
        xm = jnp.where((c // width) == (r % heads), x, 0.0)
        for t in range(t_new):
            o_ref[t:t + 1, lane0:lane0 + heads * width] = jnp.sum(
                xm[t * heads:(t + 1) * heads], axis=0, keepdims=True)

    store_own_head(o_all, vdim, 0)
    store_own_head(o_sb, sb_dim, heads * vdim)


def _decode_attn(qpad, sqb, lat, kr, sk, sv, wuk_pad, sel, wuv2, caches, page_table, *,
                 dec_batch, t_new, dims, group, layer):
    heads, vdim = dims["heads"], dims["vdim"]
    sb_heads, sb_dim = dims["sb_heads"], dims["sb_dim"]
    kv_rank, rope = dims["kv_rank"], dims["rope"]
    sb_width = sb_heads * sb_dim
    c_lat, c_kr, c_k, c_v = caches
    page = c_lat.shape[2]
    n_pages = page_table.shape[1]
    assert heads == sb_heads and n_pages % group == 0 and (n_pages // group) % 2 == 0
    assert page == LANES and t_new <= LANES
    hw = heads * LANES
    mix_w = heads * vdim + sb_width

    def new_keys(x, transpose):
        x = jnp.pad(x.reshape(dec_batch, t_new, -1), ((0, 0), (0, page - t_new), (0, 0)))
        return jnp.transpose(x, (0, 2, 1)) if transpose else x

    nlat = new_keys(lat, False)
    nkr = new_keys(kr, True)
    nk = new_keys(sk, True)
    nv = new_keys(sv, True)
    per_b = lambda b, pt: (b, 0, 0)
    const2 = lambda b, pt: (0, 0)
    kern = functools.partial(_decode_kernel, layer=layer, t_new=t_new, heads=heads, vdim=vdim,
                             sb_heads=sb_heads, sb_dim=sb_dim, page=page, group=group, n_pages=n_pages,
                             kv_rank=kv_rank)
    gk = group * page
    grid_spec = pltpu.PrefetchScalarGridSpec(
        num_scalar_prefetch=1,
        grid=(dec_batch,),
        in_specs=[
            pl.BlockSpec((None, t_new, hw), per_b),
            pl.BlockSpec((None, t_new, sb_width), per_b),
            pl.BlockSpec((None, page, kv_rank), per_b),
            pl.BlockSpec((None, rope, page), per_b),
            pl.BlockSpec((None, sb_width, page), per_b),
            pl.BlockSpec((None, sb_width, page), per_b),
            pl.BlockSpec(wuk_pad.shape, const2),
            pl.BlockSpec(sel.shape, const2),
            pl.BlockSpec(wuv2.shape, const2),
            pl.BlockSpec((LANES, LANES), const2),
            pl.BlockSpec(memory_space=pl.ANY),
            pl.BlockSpec(memory_space=pl.ANY),
            pl.BlockSpec(memory_space=pl.ANY),
            pl.BlockSpec(memory_space=pl.ANY),
        ],
        out_specs=pl.BlockSpec((None, t_new, mix_w), per_b),
        scratch_shapes=[
            pltpu.VMEM((2, gk, kv_rank), F32),
            pltpu.VMEM((2, rope, gk), F32),
            pltpu.VMEM((2, sb_width, gk), F32),
            pltpu.VMEM((2, sb_width, gk), F32),
            pltpu.SemaphoreType.DMA((4, 2)),
        ],
    )
    out = pl.pallas_call(
        kern,
        grid_spec=grid_spec,
        out_shape=jax.ShapeDtypeStruct((dec_batch, t_new, mix_w), F32),
        compiler_params=_cparams(("arbitrary",)),
        name="decode_attn",
    )(page_table, qpad.reshape(dec_batch, t_new, hw), sqb.reshape(dec_batch, t_new, sb_width),
      nlat, nkr, nk, nv, wuk_pad, sel, wuv2, _strict_lower(LANES), c_lat, c_kr, c_k, c_v)
    return out.reshape(dec_batch * t_new, mix_w)


def _finish_a_kernel(x_ref, mix_ref, wout_ref, g_ref, b_ref, wpq_ref, keys_ref,
                     h1_ref, h1t_ref, rank_ref, e1_ref, cnt_ref, sc_ref, v0_ref, v1_ref, *,
                     alpha, p_heads, n_keys, topk):
    h1 = _layernorm(alpha * x_ref[...] + _dot(mix_ref[...].astype(BF16), wout_ref[...]),
                    g_ref[...], b_ref[...])
    h1_ref[...] = h1
    h1t_ref[...] = h1.T.astype(BF16)
    qp = _dot(h1.astype(BF16), wpq_ref[...]).astype(BF16)
    qhalf = qp.shape[1] // (2 * p_heads)

    def top_vals(s, dst_ref, want_rank):
        cur = s
        rank = jnp.full(s.shape, float(topk), F32) if want_rank else None
        for r in range(topk):
            mx = jnp.max(cur, axis=0, keepdims=True)
            dst_ref[r:r + 1, :] = mx
            hit = cur == mx
            if want_rank:
                rank = jnp.where(hit, float(r), rank)
            cur = jnp.where(hit, NEG_INF, cur)
        return rank

    for h in range(p_heads):
        s0 = _dot_nt(keys_ref[2 * h], qp[:, (2 * h) * qhalf:(2 * h + 1) * qhalf])
        s1 = _dot_nt(keys_ref[2 * h + 1], qp[:, (2 * h + 1) * qhalf:(2 * h + 2) * qhalf])
        top_vals(s0, v0_ref, False)
        rank_ref[h] = _pack_rows(top_vals(s1, v1_ref, True))
        v0 = v0_ref[...]
        v1 = v1_ref[...]
        half = topk // 2
        parts = [v0[0:1, :] + v1]
        for a in range(1, half):
            parts.append(v0[a:a + 1, :] + v1[0:half, :])
        parts.append(v0[half:topk, :] + v1[0:1, :])
        cand = jnp.concatenate(parts, axis=0)
        cur = cand
        tau = None
        for r in range(topk):
            tau = jnp.max(cur, axis=0, keepdims=True)
            cur = jnp.where(cur == tau, NEG_INF, cur)
        m0 = v0[0:1, :]
        m1 = v1[0:1, :]
        zsum = jnp.sum(jnp.where(cand >= tau, jnp.exp(cand - (m0 + m1)), 0.0), axis=0, keepdims=True)
        thr = tau - s0
        cnt = jnp.zeros_like(s0)
        for b in range(topk):
            cnt = cnt + jnp.where(v1[b:b + 1, :] >= thr, 1.0, 0.0)
        cnt_ref[h] = cnt
        sc_ref[h] = jnp.exp(s0 - m0) * (1.0 / zsum)
        e1_ref[h] = _pack_rows(jnp.exp(s1 - m1))


def _finish_a(x, mix, wout, g, b, wpq, keys, *, alpha, tm):
    n, d = x.shape
    n2, n_keys, _ = keys.shape
    p_heads = n2 // 2
    row = lambda i: (i, 0)
    kern = functools.partial(_finish_a_kernel, alpha=alpha, p_heads=p_heads, n_keys=n_keys, topk=PEER_TOPK)
    return pl.pallas_call(
        kern,
        grid=(n // tm,),
        in_specs=[pl.BlockSpec((tm, d), row), pl.BlockSpec((tm, mix.shape[1]), row), _full(wout.shape),
                  _full(g.shape), _full(b.shape), _full(wpq.shape), _full(keys.shape)],
        out_specs=(pl.BlockSpec((tm, d), row),
                   pl.BlockSpec((d, tm), lambda i: (0, i)))
        + tuple(pl.BlockSpec((p_heads, rows, tm), lambda i: (0, 0, i))
                for rows in (n_keys // 2, n_keys // 2, n_keys, n_keys)),
        out_shape=(jax.ShapeDtypeStruct((n, d), F32),
                   jax.ShapeDtypeStruct((d, n), BF16),
                   jax.ShapeDtypeStruct((p_heads, n_keys // 2, n), jnp.uint32),
                   jax.ShapeDtypeStruct((p_heads, n_keys // 2, n), jnp.uint32),
                   jax.ShapeDtypeStruct((p_heads, n_keys, n), F32),
                   jax.ShapeDtypeStruct((p_heads, n_keys, n), F32)),
        scratch_shapes=[pltpu.VMEM((PEER_TOPK, tm), F32), pltpu.VMEM((PEER_TOPK, tm), F32)],
        compiler_params=_cparams(("parallel",)),
        name="finish_a",
    )(x, mix, wout, g, b, wpq, keys)


def _peer_dense_kernel(h1t_ref, rank_ref, e1_ref, cnt_ref, sc_ref, u_ref, vt_ref, o_ref, p_ref, *,
                       p_heads, n_keys, et, tt, sub):
    e = pl.program_id(1)

    @pl.when(e == 0)
    def _():
        o_ref[...] = jnp.zeros_like(o_ref)

    def key_row(rows, ii):
        packed = jnp.broadcast_to(rows[ii:ii + 1, :], (16, LANES)).astype(BF16)
        return jnp.concatenate([packed] * (n_keys // 16), axis=0)

    per_tile = et // n_keys
    inv_sqrt2 = 1.0 / math.sqrt(2.0)
    i0 = pl.multiple_of(e * per_tile, per_tile)
    h1t = h1t_ref[...]
    out = None

    def pre_act(q):
        return _dot(u_ref[q * sub:(q + 1) * sub, :], h1t)

    n_sub = et // sub
    st_next = pre_act(0)
    for q in range(n_sub):
        qs = slice(q * sub, (q + 1) * sub)
        st = st_next
        if q + 1 < n_sub:
            st_next = pre_act(q + 1)
        for c in range(tt // LANES):
            cs = slice(c * LANES, (c + 1) * LANES)
            cnt_rows = [cnt_ref[h, pl.ds(i0, per_tile), cs] for h in range(p_heads)]
            sc_rows = [sc_ref[h, pl.ds(i0, per_tile), cs] for h in range(p_heads)]
            for r in range(sub // n_keys):
                ii = q * (sub // n_keys) + r
                w = jnp.zeros((n_keys, LANES), BF16)
                for h in range(p_heads):
                    cnt = key_row(cnt_rows[h], ii)
                    sc = key_row(sc_rows[h], ii)
                    rank = pltpu.bitcast(rank_ref[h, :, cs], BF16)
                    e1 = pltpu.bitcast(e1_ref[h, :, cs], BF16)
                    w = w + jnp.where(rank < cnt, e1 * sc, jnp.zeros_like(sc))
                s = st[r * n_keys:(r + 1) * n_keys, cs]
                act = 0.5 * s * (1.0 + lax.erf(s * inv_sqrt2))
                p_ref[ii * n_keys:(ii + 1) * n_keys, cs] = act.astype(BF16) * w
        part = _dot(vt_ref[:, qs], p_ref[qs, :])
        out = part if out is None else out + part
    o_ref[...] += out


def _peer_dense(h1t, rank, e1, cnt, sc, u_b, vt_b, *, tt, et):
    d, n = h1t.shape
    p_heads, n_keys, _ = cnt.shape
    ne = u_b.shape[0]
    assert tt % LANES == 0 and et % (8 * n_keys) == 0 and ne % et == 0 and n_keys % 16 == 0
    sub = 4 * n_keys
    assert et % sub == 0
    kern = functools.partial(_peer_dense_kernel, p_heads=p_heads, n_keys=n_keys, et=et, tt=tt, sub=sub)
    route = pl.BlockSpec((p_heads, n_keys, tt), lambda i, e: (0, 0, i))
    packed = pl.BlockSpec((p_heads, n_keys // 2, tt), lambda i, e: (0, 0, i))
    return pl.pallas_call(
        kern,
        grid=(n // tt, ne // et),
        in_specs=[pl.BlockSpec((d, tt), lambda i, e: (0, i)), packed, packed, route, route,
                  pl.BlockSpec((et, d), lambda i, e: (e, 0)),
                  pl.BlockSpec((d, et), lambda i, e: (0, e))],
        out_specs=pl.BlockSpec((d, tt), lambda i, e: (0, i)),
        out_shape=jax.ShapeDtypeStruct((d, n), F32),
        scratch_shapes=[pltpu.VMEM((et, tt), BF16)],
        compiler_params=_cparams(("parallel", "arbitrary")),
        name="peer_dense",
    )(h1t, rank, e1, cnt, sc, u_b, vt_b)


def _finish_b_kernel(h1_ref, pt_ref, p_ref, g_ref, b_ref, wg_ref, bg_ref, wp_ref, y_ref, *, alpha):
    h2 = _layernorm(alpha * h1_ref[...] + pt_ref[...].T, g_ref[...], b_ref[...])
    gate = jax.nn.sigmoid(_dot(h2.astype(BF16), wg_ref[...]) + bg_ref[...])
    y_ref[...] = h2 + _dot(p_ref[...].astype(BF16), wp_ref[...]) * gate


def _finish_b(h1, peer_t, p, g, b, wg, bg, wp, *, alpha, tm):
    n, d = h1.shape
    row = lambda i: (i, 0)
    return pl.pallas_call(
        functools.partial(_finish_b_kernel, alpha=alpha),
        grid=(n // tm,),
        in_specs=[pl.BlockSpec((tm, d), row), pl.BlockSpec((d, tm), lambda i: (0, i)),
                  pl.BlockSpec((tm, p.shape[1]), row), _full(g.shape), _full(b.shape), _full(wg.shape),
                  _full(bg.shape), _full(wp.shape)],
        out_specs=pl.BlockSpec((tm, d), row),
        out_shape=jax.ShapeDtypeStruct((n, d), F32),
        compiler_params=_cparams(("parallel",)),
        name="finish_b",
    )(h1, peer_t, p, g, b, wg, bg, wp)


def _rope_table(pos, rope, nope, scale):
    half = rope // 2
    inv = ROPE_THETA ** (-jnp.arange(half, dtype=F32) / half)
    ang = pos.astype(F32)[:, None] * inv[None, :]
    cos, sin = jnp.cos(ang), jnp.sin(ang)
    n = pos.shape[0]
    zeros = lambda w: jnp.zeros((n, w), F32)
    cq = jnp.concatenate([jnp.ones((n, nope), F32), cos, cos, zeros(LANES - nope - rope)], axis=1) * scale
    sq = jnp.concatenate([zeros(nope), -sin, sin, zeros(LANES - nope - rope)], axis=1) * scale
    ck = jnp.concatenate([cos, cos, zeros(LANES - rope)], axis=1)
    sk = jnp.concatenate([-sin, sin, zeros(LANES - rope)], axis=1)
    return jnp.concatenate([cq, sq, ck, sk], axis=1)


def _swap_halves(w):
    half = w.shape[-1] // 2
    return jnp.concatenate([w[..., half:], w[..., :half]], axis=-1)


def _prep_layer(w_in, w_uq, w_uk, w_uv, dims):
    heads, nope, rope, vdim = dims["heads"], dims["nope"], dims["rope"], dims["vdim"]
    q_rank, kv_rank, sb_width = dims["q_rank"], dims["kv_rank"], dims["sb_width"]
    d = w_in.shape[0]
    cuts = np.cumsum([q_rank, kv_rank, rope, sb_width, sb_width])
    cq, ckv, kr, sq, sk, sv = jnp.split(w_in, [int(c) for c in cuts], axis=1)
    padl = lambda w: jnp.pad(w, ((0, 0), (0, LANES - w.shape[1])))
    w1 = jnp.concatenate([cq, ckv, sq, sk, sv, padl(kr), padl(_swap_halves(kr))], axis=1).astype(BF16)

    hd = nope + rope
    wq3 = w_uq.reshape(q_rank, heads, hd)
    pad3 = lambda w: jnp.pad(w, ((0, 0), (0, 0), (0, LANES - w.shape[2])))
    main = pad3(wq3)
    swap = pad3(jnp.concatenate([jnp.zeros((q_rank, heads, nope), F32), _swap_halves(wq3[..., nope:])], axis=2))
    wq = jnp.concatenate([main.reshape(q_rank, -1), swap.reshape(q_rank, -1)], axis=1).astype(BF16)

    wuk_pad = pad3(w_uk).reshape(kv_rank, heads * LANES)
    wkv = jnp.concatenate([wuk_pad, w_uv.reshape(kv_rank, heads * vdim)], axis=1).astype(BF16)

    e = np.zeros((LANES, heads * LANES), np.float32)
    sel = np.zeros((heads * LANES, LANES), np.float32)
    for h in range(heads):
        for r in range(rope):
            e[r, h * LANES + nope + r] = 1.0
            sel[h * LANES + nope + r, r] = 1.0
    return dict(w1=w1, wq=wq, wkv=wkv, emat=jnp.asarray(e, BF16), sel=jnp.asarray(sel, BF16),
                wuk_pad=wuk_pad.astype(BF16), wuv2=w_uv.reshape(kv_rank, heads * vdim).astype(BF16))


def _pick_tile(n, pref):
    t = pref
    while n % t:
        t //= 2
    return t


def kernel(x_prompt, x_sample, cache_mla_latent, cache_mla_krope, cache_sb_k, cache_sb_v, page_table,
           p_prompt, p_sample, w_in, g_q_norm, g_kv_norm, w_uq, w_uk, w_uv, w_out, ln1_g, ln1_b,
           peer_w_q, peer_sub_keys, peer_u, peer_v, ln2_g, ln2_b, ple_w_proj, ple_w_gate, ple_b_gate):
    depth = w_in.shape[0]
    batch, seq, d = x_prompt.shape
    dec_batch, t_new, _ = x_sample.shape
    n_pool, page = cache_mla_latent.shape[1:3]
    kv_rank, heads, nope = w_uk.shape[1:]
    vdim = w_uv.shape[3]
    rope = cache_mla_krope.shape[3]
    sb_heads, sb_dim = cache_sb_k.shape[3:]
    dims = dict(heads=heads, nope=nope, rope=rope, vdim=vdim, q_rank=w_uq.shape[1], kv_rank=kv_rank,
                sb_heads=sb_heads, sb_dim=sb_dim, sb_width=sb_heads * sb_dim,
                sb_scale=1.0 / math.sqrt(sb_dim))
    assert nope + rope <= LANES and rope % 2 == 0
    mla_scale = 1.0 / math.sqrt(nope + rope)
    alpha = (2 * depth) ** 0.25
    past_len = page_table.shape[1] * page
    n_p, n_s = batch * seq, dec_batch * t_new

    tab_p = _rope_table(jnp.arange(seq), rope, nope, mla_scale)
    tab_s = jnp.tile(_rope_table(past_len + jnp.arange(t_new), rope, nope, mla_scale), (dec_batch, 1))

    keys_last = lambda c: jnp.moveaxis(c, 2, -1).reshape(depth, n_pool, -1, page)
    caches = (cache_mla_latent, keys_last(cache_mla_krope), keys_last(cache_sb_k), keys_last(cache_sb_v))

    hp = x_prompt.reshape(n_p, d)
    hs = x_sample.reshape(n_s, d)
    outs_p, outs_s = [], []
    for i in range(depth):
        lw = _prep_layer(w_in[i], w_uq[i], w_uk[i], w_uv[i], dims)
        gq, gkv = g_q_norm[i][None, :], g_kv_norm[i][None, :]
        wout_b = w_out[i].astype(BF16)
        wpq_b = peer_w_q[i].astype(BF16)
        keys_b = peer_sub_keys[i].reshape(-1, *peer_sub_keys.shape[3:]).astype(BF16)
        u_b = peer_u[i].astype(BF16)
        vt_b = peer_v[i].T.astype(BF16)
        wg_b = ple_w_gate[i].astype(BF16)
        wp_b = ple_w_proj[i].astype(BF16)
        row = lambda v: v[None, :]

        def finish(h, mix, p):
            n = h.shape[0]
            tm = _pick_tile(n, 256)
            h1, h1t, rank, e1, cnt, sc = _finish_a(h, mix, wout_b, row(ln1_g[i]), row(ln1_b[i]), wpq_b,
                                                   keys_b, alpha=alpha, tm=tm)
            peer_t = _peer_dense(h1t, rank, e1, cnt, sc, u_b, vt_b, tt=_pick_tile(n, 512),
                                 et=_pick_tile(u_b.shape[0], 2048))
            return _finish_b(h1, peer_t, p.reshape(n, -1), row(ln2_g[i]), row(ln2_b[i]), wg_b,
                             row(ple_b_gate[i]), wp_b, alpha=alpha, tm=tm)

        lat, kr, sk, sv, qpad, _, _, sqb, _, _ = _inproj(
            hs, tab_s, lw["w1"], gq, gkv, lw["wq"], lw["wkv"], lw["emat"], dims=dims, tm=_pick_tile(n_s, 256))
        mix = _decode_attn(qpad, sqb, lat, kr, sk, sv, lw["wuk_pad"], lw["sel"], lw["wuv2"], caches,
                           page_table, dec_batch=dec_batch, t_new=t_new, dims=dims,
                           group=math.gcd(8, page_table.shape[1] // 2), layer=i)
        outs_s.append((lat, kr, sk, sv))
        hs = finish(hs, mix, p_sample[i])

        lat, kr, sk, sv, qpad, kpad, vb, sqb, skb, svb = _inproj(
            hp, tab_p, lw["w1"], gq, gkv, lw["wq"], lw["wkv"], lw["emat"], dims=dims, tm=_pick_tile(seq, 256))
        mix = _prompt_attn(qpad, kpad, vb, sqb, skb, svb, batch=batch, seq=seq, dims=dims,
                           blk=_pick_tile(seq, 256))
        outs_p.append((lat, kr, sk, sv))
        hp = finish(hp, mix, p_prompt[i])

    def stack(outs, k, shape):
        return jnp.stack([o[k].reshape(shape) for o in outs])

    return (hp.reshape(batch, seq, d), hs.reshape(dec_batch, t_new, d),
            stack(outs_p, 0, (batch, seq, kv_rank)), stack(outs_p, 1, (batch, seq, rope)),
            stack(outs_p, 2, (batch, seq, sb_heads, sb_dim)), stack(outs_p, 3, (batch, seq, sb_heads, sb_dim)),
            stack(outs_s, 0, (dec_batch, t_new, kv_rank)), stack(outs_s, 1, (dec_batch, t_new, rope)),
            stack(outs_s, 2, (dec_batch, t_new, sb_heads, sb_dim)),
            stack(outs_s, 3, (dec_batch, t_new, sb_heads, sb_dim)))
```
